```python
import math
import jax, jax.numpy as jnp
from jax import lax
import numpy as np

D_MODEL = 2048
BATCH = 2
SEQ = 16384
DEPTH = 4
DEC_BATCH = 1
DEC_SEQ = 8192
PAST_LEN = 128

N_MIXERS = 2
D_FF = 5632
ROPE_THETA = 10000.0
LN_EPS = 1e-5
DA_HEADS = 16
DA_HEAD_DIM = D_MODEL // DA_HEADS // 2
DA_V_DIM = 2 * DA_HEAD_DIM
DA_Q_BLOCK = 128
WG_HEADS = 16
WG_KV_HEADS = 4
WG_GROUP = WG_HEADS // WG_KV_HEADS
WG_HEAD_DIM = D_MODEL // WG_HEADS
WG_WINDOW = 128
WG_BLOCK = 128
N_A = (DEPTH + 1) // 2
N_B = DEPTH // 2
DN_ALPHA = (2.0 * DEPTH) ** 0.25
DN_BETA = (8.0 * DEPTH) ** -0.25

kernel_name = "hybrid_diffattn_swa_macaron_deepnorm_encoder"


def _layernorm(x, g, b):
    xf = x.astype(jnp.float32)
    mu = xf.mean(-1, keepdims=True)
    var = jnp.square(xf - mu).mean(-1, keepdims=True)
    y = (xf - mu) * lax.rsqrt(var + LN_EPS)
    return (y * g.astype(jnp.float32) + b.astype(jnp.float32)).astype(x.dtype)


def _rmsnorm(x, g):
    xf = x.astype(jnp.float32)
    y = xf * lax.rsqrt(jnp.mean(jnp.square(xf), -1, keepdims=True) + LN_EPS)
    return (y * g.astype(jnp.float32)).astype(x.dtype)


def _rope(x):
    s, d = x.shape[1], x.shape[-1]
    inv = ROPE_THETA ** (-jnp.arange(0, d, 2, dtype=jnp.float32) / d)
    ang = jnp.arange(s, dtype=jnp.float32)[:, None] * inv[None, :]
    ang = jnp.concatenate([ang, ang], -1)
    cos = jnp.cos(ang)[None, :, None, :].astype(x.dtype)
    sin = jnp.sin(ang)[None, :, None, :].astype(x.dtype)
    rot = jnp.concatenate([-x[..., d // 2:], x[..., :d // 2]], -1)
    return x * cos + rot * sin


def _swiglu(x, wg, wu, wd):
    return (jax.nn.silu(x @ wg) * (x @ wu)) @ wd


def _diff_attention(x, w_qkv, w_o, lq1, lk1, lq2, lk2, subln_g, lam_init):
    b, s, _ = x.shape
    nb = s // DA_Q_BLOCK
    qd = 2 * DA_HEADS * DA_HEAD_DIM
    f32 = jnp.float32
    qkv = x @ w_qkv
    q = qkv[..., :qd].reshape(b, s, 2 * DA_HEADS, DA_HEAD_DIM)
    k = qkv[..., qd:2 * qd].reshape(b, s, 2 * DA_HEADS, DA_HEAD_DIM)
    v = qkv[..., 2 * qd:].reshape(b, s, DA_HEADS, DA_V_DIM)
    q = (_rope(q) * DA_HEAD_DIM ** -0.5).reshape(b, s, DA_HEADS, 2, DA_HEAD_DIM)
    k = _rope(k).reshape(b, s, DA_HEADS, 2, DA_HEAD_DIM)
    lam = (jnp.exp(jnp.sum(lq1.astype(f32) * lk1.astype(f32)))
           - jnp.exp(jnp.sum(lq2.astype(f32) * lk2.astype(f32))) + lam_init)
    qb = jnp.moveaxis(q.reshape(b, nb, DA_Q_BLOCK, DA_HEADS, 2, DA_HEAD_DIM), 1, 0)

    def block(qblk):
        sc = jnp.einsum('bqhcd,bkhcd->bhcqk', qblk, k).astype(f32)
        p = jax.nn.softmax(sc, axis=-1)
        a = p[:, :, 0] - lam * p[:, :, 1]
        return jnp.einsum('bhqk,bkhe->bqhe', a.astype(v.dtype), v)

    o = lax.map(block, qb)
    o = jnp.moveaxis(o, 0, 1).reshape(b, s, DA_HEADS, DA_V_DIM)
    o = _rmsnorm(o, subln_g) * (1.0 - lam_init)
    return o.reshape(b, s, D_MODEL) @ w_o


def _window_gqa(x, w_qkv, w_o, sink):
    b, s, _ = x.shape
    W = WG_BLOCK
    nb = s // W
    qd = WG_HEADS * WG_HEAD_DIM
    kvd = WG_KV_HEADS * WG_HEAD_DIM
    f32 = jnp.float32
    qkv = x @ w_qkv
    q = _rope(qkv[..., :qd].reshape(b, s, WG_HEADS, WG_HEAD_DIM)) * WG_HEAD_DIM ** -0.5
    k = _rope(qkv[..., qd:qd + kvd].reshape(b, s, WG_KV_HEADS, WG_HEAD_DIM))
    v = qkv[..., qd + kvd:].reshape(b, s, WG_KV_HEADS, WG_HEAD_DIM)
    q = q.reshape(b, nb, W, WG_KV_HEADS, WG_GROUP, WG_HEAD_DIM)

    def band(t):
        t = t.reshape(b, nb, W, WG_KV_HEADS, WG_HEAD_DIM)
        t = jnp.pad(t, ((0, 0), (1, 1), (0, 0), (0, 0), (0, 0)))
        return jnp.concatenate([t[:, :-2], t[:, 1:-1], t[:, 2:]], axis=2)

    kw, vw = band(k), band(v)
    sc = jnp.einsum('bnqhgd,bnkhd->bnhgqk', q, kw).astype(f32)
    n = jnp.arange(nb)[:, None, None]
    a = jnp.arange(W)[None, :, None]
    c = jnp.arange(3 * W)[None, None, :]
    rel = c - a
    j = (n - 1) * W + c
    valid = (rel >= W - WG_WINDOW) & (rel <= W + WG_WINDOW) & (j >= 0) & (j < s)
    sc = jnp.where(valid[None, :, None, None], sc, -jnp.inf)
    sk = sink.astype(f32).reshape(1, 1, WG_KV_HEADS, WG_GROUP, 1)
    m = jnp.maximum(sc.max(-1), sk)
    e = jnp.exp(sc - m[..., None])
    p = e / (e.sum(-1, keepdims=True) + jnp.exp(sk - m)[..., None])
    o = jnp.einsum('bnhgqk,bnkhd->bnqhgd', p.astype(v.dtype), vw)
    return o.reshape(b, s, D_MODEL) @ w_o


def _trunk(x, ffn1_w_gate, ffn1_w_up, ffn1_w_down, ffn2_w_gate, ffn2_w_up, ffn2_w_down,
           ln_g, ln_b, da_w_qkv, da_w_o, da_lambda_q1, da_lambda_k1, da_lambda_q2,
           da_lambda_k2, da_subln_g, wg_w_qkv, wg_w_o, wg_sink):
    for i in range(DEPTH):
        h = _swiglu(x, ffn1_w_gate[i], ffn1_w_up[i], ffn1_w_down[i])
        x = _layernorm(DN_ALPHA * x + 0.5 * h, ln_g[i, 0], ln_b[i, 0])
        j = i // N_MIXERS
        if i % N_MIXERS == 0:
            lam_init = 0.8 - 0.6 * math.exp(-0.3 * i)
            h = _diff_attention(x, da_w_qkv[j], da_w_o[j], da_lambda_q1[j], da_lambda_k1[j],
                                da_lambda_q2[j], da_lambda_k2[j], da_subln_g[j], lam_init)
        else:
            h = _window_gqa(x, wg_w_qkv[j], wg_w_o[j], wg_sink[j])
        x = _layernorm(DN_ALPHA * x + h, ln_g[i, 1], ln_b[i, 1])
        h = _swiglu(x, ffn2_w_gate[i], ffn2_w_up[i], ffn2_w_down[i])
        x = _layernorm(DN_ALPHA * x + 0.5 * h, ln_g[i, 2], ln_b[i, 2])
    return x


def setup_inputs(seed: int = 0) -> dict:
    key = jax.random.key(seed)
    ks = jax.random.split(key, 24)
    f32 = jnp.float32

    def nrm(k, shape, scale):
        return jax.random.normal(k, shape, f32) * scale

    D, F = D_MODEL, D_FF
    da_cols = 2 * DA_HEADS * DA_HEAD_DIM
    da_scale = jnp.concatenate([jnp.ones((2 * da_cols,), f32),
                                DN_BETA * jnp.ones((DA_HEADS * DA_V_DIM,), f32)]) * D ** -0.5
    wg_q, wg_kv = WG_HEADS * WG_HEAD_DIM, WG_KV_HEADS * WG_HEAD_DIM
    wg_scale = jnp.concatenate([jnp.ones((wg_q + wg_kv,), f32),
                                DN_BETA * jnp.ones((wg_kv,), f32)]) * D ** -0.5
    return {
        "x_prompt": nrm(ks[0], (BATCH, SEQ, D), 1.0),
        "x_sample": nrm(ks[1], (DEC_BATCH, DEC_SEQ, D), 1.0),
        "ffn1_w_gate": nrm(ks[2], (DEPTH, D, F), DN_BETA * D ** -0.5),
        "ffn1_w_up": nrm(ks[3], (DEPTH, D, F), DN_BETA * D ** -0.5),
        "ffn1_w_down": nrm(ks[4], (DEPTH, F, D), DN_BETA * F ** -0.5),
        "ffn2_w_gate": nrm(ks[5], (DEPTH, D, F), DN_BETA * D ** -0.5),
        "ffn2_w_up": nrm(ks[6], (DEPTH, D, F), DN_BETA * D ** -0.5),
        "ffn2_w_down": nrm(ks[7], (DEPTH, F, D), DN_BETA * F ** -0.5),
        "ln_g": 1.0 + nrm(ks[8], (DEPTH, 3, D), 0.02),
        "ln_b": nrm(ks[9], (DEPTH, 3, D), 0.02),
        "da_w_qkv": nrm(ks[10], (N_A, D, 2 * da_cols + DA_HEADS * DA_V_DIM), 1.0) * da_scale,
        "da_w_o": nrm(ks[11], (N_A, DA_HEADS * DA_V_DIM, D), DN_BETA * D ** -0.5),
        "da_lambda_q1": nrm(ks[12], (N_A, DA_HEAD_DIM), 0.1),
        "da_lambda_k1": nrm(ks[13], (N_A, DA_HEAD_DIM), 0.1),
        "da_lambda_q2": nrm(ks[14], (N_A, DA_HEAD_DIM), 0.1),
        "da_lambda_k2": nrm(ks[15], (N_A, DA_HEAD_DIM), 0.1),
        "da_subln_g": 1.0 + nrm(ks[16], (N_A, DA_V_DIM), 0.02),
        "wg_w_qkv": nrm(ks[17], (N_B, D, wg_q + 2 * wg_kv), 1.0) * wg_scale,
        "wg_w_o": nrm(ks[18], (N_B, wg_q, D), DN_BETA * D ** -0.5),
        "wg_sink": nrm(ks[19], (N_B, WG_HEADS), 0.5),
    }


def reference(x_prompt, x_sample, ffn1_w_gate, ffn1_w_up, ffn1_w_down, ffn2_w_gate,
              ffn2_w_up, ffn2_w_down, ln_g, ln_b, da_w_qkv, da_w_o, da_lambda_q1,
              da_lambda_k1, da_lambda_q2, da_lambda_k2, da_subln_g, wg_w_qkv, wg_w_o,
              wg_sink):
    y_prompt = _trunk(x_prompt, ffn1_w_gate, ffn1_w_up, ffn1_w_down, ffn2_w_gate, ffn2_w_up,
                      ffn2_w_down, ln_g, ln_b, da_w_qkv, da_w_o, da_lambda_q1, da_lambda_k1,
                      da_lambda_q2, da_lambda_k2, da_subln_g, wg_w_qkv, wg_w_o, wg_sink)
    y_sample = _trunk(x_sample, ffn1_w_gate, ffn1_w_up, ffn1_w_down, ffn2_w_gate, ffn2_w_up,
                      ffn2_w_down, ln_g, ln_b, da_w_qkv, da_w_o, da_lambda_q1, da_lambda_k1,
                      da_lambda_q2, da_lambda_k2, da_subln_g, wg_w_qkv, wg_w_o, wg_sink)
    return (y_prompt, y_sample)
```

```python
import functools
import math

import jax
import jax.numpy as jnp
from jax import lax
from jax.experimental import pallas as pl
from jax.experimental.pallas import tpu as pltpu

F32 = jnp.float32
BF16 = jnp.bfloat16

LN_EPS = 1e-5
ROPE_THETA = 10000.0
WG_WINDOW = 128
LANES = 128
V7X_VMEM_LIMIT_BYTES = 60000 * 1024


def _cparams(n_axes):
    return pltpu.CompilerParams(
        dimension_semantics=("arbitrary",) * n_axes,
        vmem_limit_bytes=V7X_VMEM_LIMIT_BYTES,
    )


def _layernorm_rows(y, g, b):
    mu = jnp.mean(y, axis=-1, keepdims=True)
    yc = y - mu
    var = jnp.mean(yc * yc, axis=-1, keepdims=True)
    return yc * lax.rsqrt(var + LN_EPS) * g + b


def _ffn_kernel(x_ref, wg_ref, wu_ref, wd_ref, g_ref, b_ref, o_ref, xb_ref, acc_ref, *, alpha, nf):
    f = pl.program_id(1)

    @pl.when(f == 0)
    def _():
        xb_ref[...] = x_ref[...].astype(BF16)

    xb = xb_ref[...]
    hg = jnp.dot(xb, wg_ref[...], preferred_element_type=F32)
    hu = jnp.dot(xb, wu_ref[...], preferred_element_type=F32)
    a = (hg * jax.nn.sigmoid(hg) * hu).astype(BF16)
    part = jnp.dot(a, wd_ref[...], preferred_element_type=F32)

    @pl.when(f == 0)
    def _():
        acc_ref[...] = part

    @pl.when(f > 0)
    def _():
        acc_ref[...] += part

    @pl.when(f == nf - 1)
    def _():
        y = alpha * x_ref[...] + 0.5 * acc_ref[...]
        o_ref[...] = _layernorm_rows(y, g_ref[...], b_ref[...])


def _ffn(x, wg, wu, wd, g, b, *, alpha, tm, tf):
    t, d = x.shape
    f = wg.shape[1]
    nf = f // tf
    return pl.pallas_call(
        functools.partial(_ffn_kernel, alpha=alpha, nf=nf),
        grid=(t // tm, nf),
        in_specs=[
            pl.BlockSpec((tm, d), lambda i, j: (i, 0)),
            pl.BlockSpec((d, tf), lambda i, j: (0, j)),
            pl.BlockSpec((d, tf), lambda i, j: (0, j)),
            pl.BlockSpec((tf, d), lambda i, j: (j, 0)),
            pl.BlockSpec((1, d), lambda i, j: (0, 0)),
            pl.BlockSpec((1, d), lambda i, j: (0, 0)),
        ],
        out_specs=pl.BlockSpec((tm, d), lambda i, j: (i, 0)),
        out_shape=jax.ShapeDtypeStruct((t, d), F32),
        scratch_shapes=[pltpu.VMEM((tm, d), BF16), pltpu.VMEM((tm, d), F32)],
        compiler_params=_cparams(2),
        name="ffn_ln",
    )(x, wg, wu, wd, g, b)


def _rope_tables(max_len, head_dim):
    inv = ROPE_THETA ** (-jnp.arange(0, head_dim, 2, dtype=F32) / head_dim)
    ang = jnp.arange(max_len, dtype=F32)[:, None] * inv[None, :]
    ang = jnp.concatenate([ang, ang], -1)
    cos = jnp.cos(ang)
    sin = jnp.sin(ang)
    half = head_dim // 2
    sign = jnp.where(jnp.arange(head_dim) < half, -1.0, 1.0).astype(F32)
    sin = sin * sign[None, :]
    reps = LANES // head_dim
    return jnp.tile(cos, (1, reps)), jnp.tile(sin, (1, reps))


def _rope_lane_tile(y, cos, sin_signed, head_dim):
    half = head_dim // 2
    lane = lax.broadcasted_iota(jnp.int32, y.shape, 1)
    lo = (lane & (head_dim - 1)) < half
    fwd = pltpu.roll(y, half, 1)
    bwd = pltpu.roll(y, LANES - half, 1)
    return y * cos + jnp.where(lo, bwd, fwd) * sin_signed


def _proj_rope_kernel(x_ref, w_ref, cos_ref, sin_ref, o_ref, *, head_dim, n_scaled, scale):
    y = jnp.dot(x_ref[...].astype(BF16), w_ref[...], preferred_element_type=F32)
    cos = cos_ref[...]
    sin = sin_ref[...]
    j = pl.program_id(1)
    tiles_per_block = y.shape[1] // LANES
    for c in range(tiles_per_block):
        r = _rope_lane_tile(y[:, c * LANES:(c + 1) * LANES], cos, sin, head_dim)
        tile_idx = j * tiles_per_block + c
        r = r * jnp.where(tile_idx < n_scaled, scale, 1.0).astype(F32)
        o_ref[:, c * LANES:(c + 1) * LANES] = r.astype(o_ref.dtype)


def _pos_block(i, tm, seq_layout):
    (n_a, len_a), (_, len_b) = seq_layout
    tiles_a = n_a * len_a // tm
    return jnp.where(i < tiles_a, i % (len_a // tm), (i - tiles_a) % (len_b // tm))


def _proj_rope(x, w, cos, sin, *, head_dim, n_scaled, scale, tm, tn, seq_layout):
    t, d = x.shape
    n = w.shape[1]
    pos_spec = pl.BlockSpec((tm, LANES), lambda i, j: (_pos_block(i, tm, seq_layout), 0))
    return pl.pallas_call(
        functools.partial(_proj_rope_kernel, head_dim=head_dim, n_scaled=n_scaled, scale=scale),
        grid=(t // tm, n // tn),
        in_specs=[
            pl.BlockSpec((tm, d), lambda i, j: (i, 0)),
            pl.BlockSpec((d, tn), lambda i, j: (0, j)),
            pos_spec,
            pos_spec,
        ],
        out_specs=pl.BlockSpec((tm, tn), lambda i, j: (i, j)),
        out_shape=jax.ShapeDtypeStruct((t, n), BF16),
        compiler_params=_cparams(2),
        name="proj_rope",
    )(x, w, cos, sin)


def _proj_kernel(x_ref, w_ref, o_ref):
    o_ref[...] = jnp.dot(x_ref[...].astype(BF16), w_ref[...],
                         preferred_element_type=F32).astype(o_ref.dtype)


def _proj(x, w, *, tm):
    t, d = x.shape
    n = w.shape[1]
    return pl.pallas_call(
        _proj_kernel,
        grid=(t // tm,),
        in_specs=[
            pl.BlockSpec((tm, d), lambda i: (i, 0)),
            pl.BlockSpec((d, n), lambda i: (0, 0)),
        ],
        out_specs=pl.BlockSpec((tm, n), lambda i: (i, 0)),
        out_shape=jax.ShapeDtypeStruct((t, n), BF16),
        compiler_params=_cparams(1),
        name="proj",
    )(x, w)


def _proj_t_kernel(x_ref, wt_ref, o_ref, *, tk):
    yt = lax.dot_general(wt_ref[...], x_ref[...].astype(BF16), (((1,), (1,)), ((), ())),
                         preferred_element_type=F32)
    for c in range(o_ref.shape[0]):
        o_ref[c] = yt[:, c * tk:(c + 1) * tk].astype(o_ref.dtype)


def _proj_t(x, wt, *, tm, tk):
    t, d = x.shape
    n = wt.shape[0]
    return pl.pallas_call(
        functools.partial(_proj_t_kernel, tk=tk),
        grid=(t // tm,),
        in_specs=[
            pl.BlockSpec((tm, d), lambda i: (i, 0)),
            pl.BlockSpec((n, d), lambda i: (0, 0)),
        ],
        out_specs=pl.BlockSpec((tm // tk, n, tk), lambda i: (i, 0, 0)),
        out_shape=jax.ShapeDtypeStruct((t // tk, n, tk), BF16),
        compiler_params=_cparams(1),
        name="proj_t",
    )(x, wt)


def _diff_attn_kernel(q_ref, k_ref, vt_ref, lq1_ref, lk1_ref, lq2_ref, lk2_ref, g_ref, o_ref,
                      qb_ref, m_ref, l_ref, acc_ref, *, tq, tk, nk, hd, lam_init):
    q = q_ref[...]
    lane = lax.broadcasted_iota(jnp.int32, q.shape, 1)
    zero = jnp.zeros_like(q)
    qb_ref[0:tq, :] = jnp.where(lane < hd, q, zero)
    qb_ref[tq:2 * tq, :] = jnp.where(lane >= hd, q, zero)
    m_ref[...] = jnp.full(m_ref.shape, -jnp.inf, F32)
    l_ref[...] = jnp.zeros(l_ref.shape, F32)
    acc_ref[...] = jnp.zeros(acc_ref.shape, F32)

    def chunk(i, carry):
        off = pl.multiple_of(i * tk, tk)
        k = k_ref[pl.ds(off, tk), :]
        vt = vt_ref[i]
        s = lax.dot_general(k, qb_ref[...], (((1,), (1,)), ((), ())),
                            preferred_element_type=F32)
        for c in range(2):
            sc = s[:, c * tq:(c + 1) * tq]
            m_old = m_ref[c]
            m_new = jnp.maximum(m_old, jnp.max(sc, axis=0, keepdims=True))
            a = jnp.exp(m_old - m_new)
            p = jnp.exp(sc - m_new)
            l_ref[c] = a * l_ref[c] + jnp.sum(p, axis=0, keepdims=True)
            acc_ref[c] = a * acc_ref[c] + jnp.dot(vt, p.astype(BF16), preferred_element_type=F32)
            m_ref[c] = m_new
        return carry

    lax.fori_loop(0, nk, chunk, 0)

    lam = (jnp.exp(jnp.sum(lq1_ref[...] * lk1_ref[...], axis=-1, keepdims=True))
           - jnp.exp(jnp.sum(lq2_ref[...] * lk2_ref[...], axis=-1, keepdims=True)) + lam_init)
    o = acc_ref[0] * (1.0 / l_ref[0]) - lam * (acc_ref[1] * (1.0 / l_ref[1]))
    ms = jnp.mean(o * o, axis=0, keepdims=True)
    o = o * lax.rsqrt(ms + LN_EPS) * g_ref[...] * (1.0 - lam_init)
    o_ref[...] = o.T.astype(o_ref.dtype)


def _diff_attn(qk, vt, lq1, lk1, lq2, lk2, g_col, *, row0, n_seq, seq_len, n_heads, tq, tk, lam_init,
               out_rows):
    dv = g_col.shape[0]
    hd = lq1.shape[1]
    nq = seq_len // tq
    nk = seq_len // tk
    q_blk0 = row0 // tq
    s_blk0 = row0 // seq_len
    kernel = functools.partial(_diff_attn_kernel, tq=tq, tk=tk, nk=nk, hd=hd, lam_init=lam_init)
    vec = pl.BlockSpec((1, hd), lambda b, h, i: (0, 0))
    return pl.pallas_call(
        kernel,
        grid=(n_seq, n_heads, nq),
        in_specs=[
            pl.BlockSpec((tq, 2 * hd), lambda b, h, i: (q_blk0 + b * nq + i, h)),
            pl.BlockSpec((seq_len, 2 * hd), lambda b, h, i: (s_blk0 + b, n_heads + h)),
            pl.BlockSpec((nk, dv, tk), lambda b, h, i: (s_blk0 + b, h, 0)),
            vec, vec, vec, vec,
            pl.BlockSpec((dv, 1), lambda b, h, i: (0, 0)),
        ],
        out_specs=pl.BlockSpec((tq, dv), lambda b, h, i: (b * nq + i, h)),
        out_shape=jax.ShapeDtypeStruct((out_rows, n_heads * dv), BF16),
        scratch_shapes=[
            pltpu.VMEM((2 * tq, 2 * hd), BF16),
            pltpu.VMEM((2, 1, tq), F32),
            pltpu.VMEM((2, 1, tq), F32),
            pltpu.VMEM((2, dv, tq), F32),
        ],
        compiler_params=_cparams(3),
        name="diff_attn",
    )(qk, qk, vt, lq1, lk1, lq2, lk2, g_col)


def _window_attn_kernel(sink_ref, q_ref, kv_ref, kvp_ref, kvn_ref, o_ref, win_ref,
                        *, tq, n_kv, group, hd, seq_layout):
    w = WG_WINDOW
    g0 = pl.program_id(0) * tq
    (n_a, len_a), (_, len_b) = seq_layout
    in_a = g0 < n_a * len_a
    seq_len = jnp.where(in_a, jnp.int32(len_a), jnp.int32(len_b))
    pos0 = jnp.where(in_a, lax.rem(g0, jnp.int32(len_a)), lax.rem(g0 - n_a * len_a, jnp.int32(len_b)))

    win_ref[0:w, :] = kvp_ref[...]
    win_ref[w:w + tq, :] = kv_ref[...]
    win_ref[w + tq:w + tq + w, :] = kvn_ref[...]

    rows = group * w
    a_idx = lax.broadcasted_iota(jnp.int32, (rows, 3 * w), 0) & (w - 1)
    c_idx = lax.broadcasted_iota(jnp.int32, (rows, 3 * w), 1)
    rel = c_idx - a_idx
    band = (rel >= 0) & (rel <= 2 * w)
    head_in_group = lax.broadcasted_iota(jnp.int32, (rows, 1), 0) // w

    for n in range(tq // w):
        kpos = pos0 + (n - 1) * w + c_idx
        valid = band & (kpos >= 0) & (kpos < seq_len)
        for g in range(n_kv):
            kw = win_ref[n * w:n * w + 3 * w, g * hd:(g + 1) * hd]
            vw = win_ref[n * w:n * w + 3 * w, (n_kv + g) * hd:(n_kv + g + 1) * hd]
            qg = jnp.concatenate(
                [q_ref[n * w:(n + 1) * w, (g * group + j) * hd:(g * group + j + 1) * hd]
                 for j in range(group)], axis=0)
            s = lax.dot_general(qg, kw, (((1,), (1,)), ((), ())), preferred_element_type=F32)
            s = jnp.where(valid, s, -jnp.inf)
            sk = jnp.zeros((rows, 1), F32)
            for j in range(group):
                sk = jnp.where(head_in_group == j, sink_ref[g * group + j], sk)
            m = jnp.maximum(jnp.max(s, axis=-1, keepdims=True), sk)
            e = jnp.exp(s - m)
            den = jnp.sum(e, axis=-1, keepdims=True) + jnp.exp(sk - m)
            p = (e * (1.0 / den)).astype(BF16)
            o = jnp.dot(p, vw, preferred_element_type=F32)
            for j in range(group):
                h = g * group + j
                o_ref[n * w:(n + 1) * w, h * hd:(h + 1) * hd] = o[j * w:(j + 1) * w, :].astype(o_ref.dtype)


def _window_attn(q, kv, sink, *, n_heads, n_kv, tq, seq_layout):
    t, qd = q.shape
    hd = qd // n_heads
    w = WG_WINDOW
    r = tq // w
    n_blk = t // w
    kernel = functools.partial(_window_attn_kernel, tq=tq, n_kv=n_kv, group=n_heads // n_kv, hd=hd,
                               seq_layout=seq_layout)
    return pl.pallas_call(
        kernel,
        grid=(t // tq,),
        in_specs=[
            pl.BlockSpec(memory_space=pltpu.SMEM),
            pl.BlockSpec((tq, qd), lambda i: (i, 0)),
            pl.BlockSpec((tq, 2 * n_kv * hd), lambda i: (i, 0)),
            pl.BlockSpec((w, 2 * n_kv * hd), lambda i: (jnp.maximum(i * r - 1, 0), 0)),
            pl.BlockSpec((w, 2 * n_kv * hd), lambda i: (jnp.minimum((i + 1) * r, n_blk - 1), 0)),
        ],
        out_specs=pl.BlockSpec((tq, qd), lambda i: (i, 0)),
        scratch_shapes=[pltpu.VMEM((tq + 2 * w, 2 * n_kv * hd), BF16)],
        out_shape=jax.ShapeDtypeStruct((t, qd), BF16),
        compiler_params=_cparams(1),
        name="window_attn",
    )(sink, q, kv, kv, kv)


def _outproj_kernel(x_ref, o_ref, w_ref, g_ref, b_ref, y_ref, *, alpha):
    h = jnp.dot(o_ref[...], w_ref[...], preferred_element_type=F32)
    y_ref[...] = _layernorm_rows(alpha * x_ref[...] + h, g_ref[...], b_ref[...])


def _outproj(x, o, w, g, b, *, alpha, tm):
    t, d = x.shape
    k = o.shape[1]
    return pl.pallas_call(
        functools.partial(_outproj_kernel, alpha=alpha),
        grid=(t // tm,),
        in_specs=[
            pl.BlockSpec((tm, d), lambda i: (i, 0)),
            pl.BlockSpec((tm, k), lambda i: (i, 0)),
            pl.BlockSpec((k, d), lambda i: (0, 0)),
            pl.BlockSpec((1, d), lambda i: (0, 0)),
            pl.BlockSpec((1, d), lambda i: (0, 0)),
        ],
        out_specs=pl.BlockSpec((tm, d), lambda i: (i, 0)),
        out_shape=jax.ShapeDtypeStruct((t, d), F32),
        compiler_params=_cparams(1),
        name="outproj_ln",
    )(x, o, w, g, b)


def _largest_divisor(n, cap, multiple):
    best = None
    for c in range(multiple, min(n, cap) + 1, multiple):
        if n % c == 0:
            best = c
    assert best is not None, (n, cap, multiple)
    return best


def _tiles(seq_lens, d_ff):
    common = math.gcd(*seq_lens)
    return dict(
        tm=_largest_divisor(common, 512, LANES),
        tf=_largest_divisor(d_ff, 512, 2 * LANES),
        tq=_largest_divisor(common, 512, LANES),
        tk=_largest_divisor(common, 256, LANES),
        tw=_largest_divisor(common, 512, WG_WINDOW),
    )


def kernel(x_prompt, x_sample, ffn1_w_gate, ffn1_w_up, ffn1_w_down, ffn2_w_gate, ffn2_w_up, ffn2_w_down, ln_g, ln_b, da_w_qkv, da_w_o, da_lambda_q1, da_lambda_k1, da_lambda_q2, da_lambda_k2, da_subln_g, wg_w_qkv, wg_w_o, wg_sink):
    bp, sp, d = x_prompt.shape
    bs, ss, _ = x_sample.shape
    depth = ffn1_w_gate.shape[0]
    d_ff = ffn1_w_gate.shape[2]
    alpha = (2.0 * depth) ** 0.25

    da_hd = da_lambda_q1.shape[1]
    da_dv = da_subln_g.shape[1]
    da_heads = d // da_dv
    da_qd = 2 * da_heads * da_hd
    wg_heads = wg_sink.shape[1]
    wg_hd = d // wg_heads
    wg_kv = (wg_w_qkv.shape[2] - d) // (2 * wg_hd)
    assert 2 * da_hd == LANES and da_dv == LANES and wg_hd == LANES
    assert (bp * sp) % ss == 0 and sp % ss == 0

    tl = _tiles((sp, ss), d_ff)
    tm, tf, tq, tk, tw = tl["tm"], tl["tf"], tl["tq"], tl["tk"], tl["tw"]
    t_prompt = bp * sp
    seq_layout = ((bp, sp), (bs, ss))
    cos_a, sin_a = _rope_tables(max(sp, ss), da_hd)
    cos_b, sin_b = _rope_tables(max(sp, ss), wg_hd)

    x = jnp.concatenate([x_prompt.reshape(t_prompt, d), x_sample.reshape(bs * ss, d)], axis=0)

    def ln(i, k):
        return ln_g[i, k][None, :], ln_b[i, k][None, :]

    def ffn(x, wg, wu, wd, g, b):
        return _ffn(x, wg.astype(BF16), wu.astype(BF16), wd.astype(BF16), g, b, alpha=alpha, tm=tm, tf=tf)

    for i in range(depth):
        x = ffn(x, ffn1_w_gate[i], ffn1_w_up[i], ffn1_w_down[i], *ln(i, 0))
        j = i // 2
        if i % 2 == 0:
            lam_init = 0.8 - 0.6 * math.exp(-0.3 * i)
            w = da_w_qkv[j]
            qk = _proj_rope(x, w[:, :2 * da_qd].astype(BF16), cos_a, sin_a, head_dim=da_hd,
                            n_scaled=da_qd // LANES, scale=da_hd ** -0.5, tm=tm, tn=da_qd // 2,
                            seq_layout=seq_layout)
            vt = _proj_t(x, w[:, 2 * da_qd:].T.astype(BF16), tm=tm, tk=tk)
            lam_args = (da_lambda_q1[j][None, :], da_lambda_k1[j][None, :],
                        da_lambda_q2[j][None, :], da_lambda_k2[j][None, :], da_subln_g[j][:, None])
            o_p = _diff_attn(qk, vt, *lam_args, row0=0, n_seq=bp, seq_len=sp, n_heads=da_heads,
                             tq=tq, tk=tk, lam_init=lam_init, out_rows=t_prompt)
            o_s = _diff_attn(qk, vt, *lam_args, row0=t_prompt, n_seq=bs, seq_len=ss, n_heads=da_heads,
                             tq=tq, tk=tk, lam_init=lam_init, out_rows=bs * ss)
            o = jnp.concatenate([o_p, o_s], axis=0)
            w_o = da_w_o[j]
        else:
            w = wg_w_qkv[j].astype(BF16)
            q = _proj_rope(x, w[:, :d], cos_b, sin_b, head_dim=wg_hd, n_scaled=d // LANES,
                           scale=wg_hd ** -0.5, tm=tm, tn=d // 2, seq_layout=seq_layout)
            k = _proj_rope(x, w[:, d:d + wg_kv * wg_hd], cos_b, sin_b, head_dim=wg_hd, n_scaled=0,
                           scale=1.0, tm=tm, tn=wg_kv * wg_hd, seq_layout=seq_layout)
            v = _proj(x, w[:, d + wg_kv * wg_hd:], tm=tm)
            kv = jnp.concatenate([k, v], axis=1)
            o = _window_attn(q, kv, wg_sink[j], n_heads=wg_heads, n_kv=wg_kv, tq=tw, seq_layout=seq_layout)
            w_o = wg_w_o[j]
        x = _outproj(x, o, w_o.astype(BF16), *ln(i, 1), alpha=alpha, tm=tm)
        x = ffn(x, ffn2_w_gate[i], ffn2_w_up[i], ffn2_w_down[i], *ln(i, 2))

    y_prompt = x[:t_prompt].reshape(bp, sp, d)
    y_sample = x[t_prompt:].reshape(bs, ss, d)
    return (y_prompt, y_sample)
```

```python
import functools
import math

import jax
import jax.numpy as jnp
from jax import lax
from jax.experimental import pallas as pl
from jax.experimental.pallas import tpu as pltpu

F32 = jnp.float32
BF16 = jnp.bfloat16

LN_EPS = 1e-5
ROPE_THETA = 10000.0
WG_WINDOW = 128
LANES = 128
V7X_VMEM_LIMIT_BYTES = 60000 * 1024


def _cparams(n_axes):
    return pltpu.CompilerParams(
        dimension_semantics=("arbitrary",) * n_axes,
        vmem_limit_bytes=V7X_VMEM_LIMIT_BYTES,
    )


def _layernorm_rows(y, g, b):
    mu = jnp.mean(y, axis=-1, keepdims=True)
    yc = y - mu
    var = jnp.mean(yc * yc, axis=-1, keepdims=True)
    return yc * lax.rsqrt(var + LN_EPS) * g + b


def _ffn_kernel(x_ref, wg_ref, wu_ref, wd_ref, g_ref, b_ref, o_ref, xb_ref, acc_ref, *, alpha, nf):
    f = pl.program_id(1)

    @pl.when(f == 0)
    def _():
        xb_ref[...] = x_ref[...].astype(BF16)

    xb = xb_ref[...]
    hg = jnp.dot(xb, wg_ref[...], preferred_element_type=F32)
    hu = jnp.dot(xb, wu_ref[...], preferred_element_type=F32)
    a = (hg * jax.nn.sigmoid(hg) * hu).astype(BF16)
    part = jnp.dot(a, wd_ref[...], preferred_element_type=F32)

    @pl.when(f == 0)
    def _():
        acc_ref[...] = part

    @pl.when(f > 0)
    def _():
        acc_ref[...] += part

    @pl.when(f == nf - 1)
    def _():
        y = alpha * x_ref[...] + 0.5 * acc_ref[...]
        o_ref[...] = _layernorm_rows(y, g_ref[...], b_ref[...])


def _ffn(x, wg, wu, wd, g, b, *, alpha, tm, tf):
    t, d = x.shape
    f = wg.shape[1]
    nf = f // tf
    return pl.pallas_call(
        functools.partial(_ffn_kernel, alpha=alpha, nf=nf),
        grid=(t // tm, nf),
        in_specs=[
            pl.BlockSpec((tm, d), lambda i, j: (i, 0)),
            pl.BlockSpec((d, tf), lambda i, j: (0, j)),
            pl.BlockSpec((d, tf), lambda i, j: (0, j)),
            pl.BlockSpec((tf, d), lambda i, j: (j, 0)),
            pl.BlockSpec((1, d), lambda i, j: (0, 0)),
            pl.BlockSpec((1, d), lambda i, j: (0, 0)),
        ],
        out_specs=pl.BlockSpec((tm, d), lambda i, j: (i, 0)),
        out_shape=jax.ShapeDtypeStruct((t, d), F32),
        scratch_shapes=[pltpu.VMEM((tm, d), BF16), pltpu.VMEM((tm, d), F32)],
        compiler_params=_cparams(2),
        name="ffn_ln",
    )(x, wg, wu, wd, g, b)


def _rope_tables(max_len, head_dim):
    inv = ROPE_THETA ** (-jnp.arange(0, head_dim, 2, dtype=F32) / head_dim)
    ang = jnp.arange(max_len, dtype=F32)[:, None] * inv[None, :]
    ang = jnp.concatenate([ang, ang], -1)
    cos = jnp.cos(ang)
    sin = jnp.sin(ang)
    half = head_dim // 2
    sign = jnp.where(jnp.arange(head_dim) < half, -1.0, 1.0).astype(F32)
    sin = sin * sign[None, :]
    reps = LANES // head_dim
    return jnp.tile(cos, (1, reps)), jnp.tile(sin, (1, reps))


def _rope_lane_tile(y, cos, sin_signed, head_dim):
    half = head_dim // 2
    lane = lax.broadcasted_iota(jnp.int32, y.shape, 1)
    lo = (lane & (head_dim - 1)) < half
    fwd = pltpu.roll(y, half, 1)
    bwd = pltpu.roll(y, LANES - half, 1)
    return y * cos + jnp.where(lo, bwd, fwd) * sin_signed


def _proj_rope_kernel(x_ref, w_ref, cos_ref, sin_ref, o_ref, *, head_dim, n_scaled, scale):
    y = jnp.dot(x_ref[...].astype(BF16), w_ref[...], preferred_element_type=F32)
    cos = cos_ref[...]
    sin = sin_ref[...]
    j = pl.program_id(1)
    tiles_per_block = y.shape[1] // LANES
    for c in range(tiles_per_block):
        r = _rope_lane_tile(y[:, c * LANES:(c + 1) * LANES], cos, sin, head_dim)
        tile_idx = j * tiles_per_block + c
        r = r * jnp.where(tile_idx < n_scaled, scale, 1.0).astype(F32)
        o_ref[:, c * LANES:(c + 1) * LANES] = r.astype(o_ref.dtype)


def _pos_block(i, tm, seq_layout):
    (n_a, len_a), (_, len_b) = seq_layout
    tiles_a = n_a * len_a // tm
    return jnp.where(i < tiles_a, i % (len_a // tm), (i - tiles_a) % (len_b // tm))


def _proj_rope(x, w, cos, sin, *, head_dim, n_scaled, scale, tm, tn, seq_layout):
    t, d = x.shape
    n = w.shape[1]
    pos_spec = pl.BlockSpec((tm, LANES), lambda i, j: (_pos_block(i, tm, seq_layout), 0))
    return pl.pallas_call(
        functools.partial(_proj_rope_kernel, head_dim=head_dim, n_scaled=n_scaled, scale=scale),
        grid=(t // tm, n // tn),
        in_specs=[
            pl.BlockSpec((tm, d), lambda i, j: (i, 0)),
            pl.BlockSpec((d, tn), lambda i, j: (0, j)),
            pos_spec,
            pos_spec,
        ],
        out_specs=pl.BlockSpec((tm, tn), lambda i, j: (i, j)),
        out_shape=jax.ShapeDtypeStruct((t, n), BF16),
        compiler_params=_cparams(2),
        name="proj_rope",
    )(x, w, cos, sin)


def _proj_kernel(x_ref, w_ref, o_ref):
    o_ref[...] = jnp.dot(x_ref[...].astype(BF16), w_ref[...],
                         preferred_element_type=F32).astype(o_ref.dtype)


def _proj(x, w, *, tm):
    t, d = x.shape
    n = w.shape[1]
    return pl.pallas_call(
        _proj_kernel,
        grid=(t // tm,),
        in_specs=[
            pl.BlockSpec((tm, d), lambda i: (i, 0)),
            pl.BlockSpec((d, n), lambda i: (0, 0)),
        ],
        out_specs=pl.BlockSpec((tm, n), lambda i: (i, 0)),
        out_shape=jax.ShapeDtypeStruct((t, n), BF16),
        compiler_params=_cparams(1),
        name="proj",
    )(x, w)


def _proj_t_kernel(x_ref, wt_ref, o_ref, *, tk):
    yt = lax.dot_general(wt_ref[...], x_ref[...].astype(BF16), (((1,), (1,)), ((), ())),
                         preferred_element_type=F32)
    for c in range(o_ref.shape[0]):
        o_ref[c] = yt[:, c * tk:(c + 1) * tk].astype(o_ref.dtype)


def _proj_t(x, wt, *, tm, tk):
    t, d = x.shape
    n = wt.shape[0]
    return pl.pallas_call(
        functools.partial(_proj_t_kernel, tk=tk),
        grid=(t // tm,),
        in_specs=[
            pl.BlockSpec((tm, d), lambda i: (i, 0)),
            pl.BlockSpec((n, d), lambda i: (0, 0)),
        ],
        out_specs=pl.BlockSpec((tm // tk, n, tk), lambda i: (i, 0, 0)),
        out_shape=jax.ShapeDtypeStruct((t // tk, n, tk), BF16),
        compiler_params=_cparams(1),
        name="proj_t",
    )(x, wt)


ONES_ROWS = 16


def _diff_attn_kernel(q_ref, k_ref, vt_ref, lq1_ref, lk1_ref, lq2_ref, lk2_ref, g_ref, o_ref,
                      qb_ref, s_ref, p_ref, a_ref, m_ref, acc_ref, *, tq, tk, nk, hd, dv, lam_init):
    q = q_ref[...]
    lane = lax.broadcasted_iota(jnp.int32, q.shape, 1)
    zero = jnp.zeros_like(q)
    qb_ref[0:tq, :] = jnp.where(lane < hd, q, zero)
    qb_ref[tq:2 * tq, :] = jnp.where(lane >= hd, q, zero)
    ones = jnp.ones((ONES_ROWS, tk), BF16)

    def scores(i, slot):
        off = pl.multiple_of(i * tk, tk)
        k = k_ref[pl.ds(off, tk), :]
        s_ref[slot] = lax.dot_general(k, qb_ref[...], (((1,), (1,)), ((), ())),
                                      preferred_element_type=F32)

    def softmax(slot):
        for c in range(2):
            m_old = m_ref[c]
            m_new = jnp.maximum(m_old, jnp.max(s_ref[slot, :, c * tq:(c + 1) * tq], axis=0, keepdims=True))
            a_ref[slot, c] = jnp.exp2(m_old - m_new)
            p_ref[slot, c] = jnp.exp2(s_ref[slot, :, c * tq:(c + 1) * tq] - m_new).astype(BF16)
            m_ref[c] = m_new

    def values(i, slot):
        vt = jnp.concatenate([vt_ref[i], ones], axis=0)
        for c in range(2):
            acc_ref[c] = a_ref[slot, c] * acc_ref[c] + jnp.dot(vt, p_ref[slot, c],
                                                               preferred_element_type=F32)

    m_ref[...] = jnp.full(m_ref.shape, -jnp.inf, F32)
    acc_ref[...] = jnp.zeros(acc_ref.shape, F32)
    p_ref[1] = jnp.zeros(p_ref.shape[1:], BF16)
    a_ref[1] = jnp.ones(a_ref.shape[1:], F32)
    scores(0, 0)

    def pair(j, carry):
        i = 2 * j
        scores(i + 1, 1)
        softmax(0)
        values(jnp.maximum(i - 1, 0), 1)
        scores(jnp.minimum(i + 2, nk - 1), 0)
        softmax(1)
        values(i, 0)
        return carry

    lax.fori_loop(0, nk // 2, pair, 0, unroll=2)
    values(nk - 1, 1)

    lam = (jnp.exp(jnp.sum(lq1_ref[...] * lk1_ref[...], axis=-1, keepdims=True))
           - jnp.exp(jnp.sum(lq2_ref[...] * lk2_ref[...], axis=-1, keepdims=True)) + lam_init)
    o = (acc_ref[0, 0:dv] * (1.0 / acc_ref[0, dv:dv + 1])
         - lam * (acc_ref[1, 0:dv] * (1.0 / acc_ref[1, dv:dv + 1])))
    ms = jnp.mean(o * o, axis=0, keepdims=True)
    o = o * lax.rsqrt(ms + LN_EPS) * g_ref[...] * (1.0 - lam_init)
    o_ref[...] = o.T.astype(o_ref.dtype)


def _diff_attn(qk, vt, lq1, lk1, lq2, lk2, g_col, *, row0, n_seq, seq_len, n_heads, tq, tk, lam_init,
               out_rows):
    dv = g_col.shape[0]
    hd = lq1.shape[1]
    nq = seq_len // tq
    nk = seq_len // tk
    assert nk % 2 == 0, "the chunk loop is software-pipelined in pairs"
    q_blk0 = row0 // tq
    s_blk0 = row0 // seq_len
    kernel = functools.partial(_diff_attn_kernel, tq=tq, tk=tk, nk=nk, hd=hd, dv=dv, lam_init=lam_init)
    vec = pl.BlockSpec((1, hd), lambda b, h, i: (0, 0))
    return pl.pallas_call(
        kernel,
        grid=(n_seq, n_heads, nq),
        in_specs=[
            pl.BlockSpec((tq, 2 * hd), lambda b, h, i: (q_blk0 + b * nq + i, h)),
            pl.BlockSpec((seq_len, 2 * hd), lambda b, h, i: (s_blk0 + b, n_heads + h)),
            pl.BlockSpec((nk, dv, tk), lambda b, h, i: (s_blk0 + b, h, 0)),
            vec, vec, vec, vec,
            pl.BlockSpec((dv, 1), lambda b, h, i: (0, 0)),
        ],
        out_specs=pl.BlockSpec((tq, dv), lambda b, h, i: (b * nq + i, h)),
        out_shape=jax.ShapeDtypeStruct((out_rows, n_heads * dv), BF16),
        scratch_shapes=[
            pltpu.VMEM((2 * tq, 2 * hd), BF16),
            pltpu.VMEM((2, tk, 2 * tq), F32),
            pltpu.VMEM((2, 2, tk, tq), BF16),
            pltpu.VMEM((2, 2, 1, tq), F32),
            pltpu.VMEM((2, 1, tq), F32),
            pltpu.VMEM((2, dv + ONES_ROWS, tq), F32),
        ],
        compiler_params=_cparams(3),
        name="diff_attn",
    )(qk, qk, vt, lq1, lk1, lq2, lk2, g_col)


def _window_attn_kernel(sink_ref, q_ref, kv_ref, kvp_ref, kvn_ref, o_ref, win_ref,
                        *, tq, n_kv, group, hd, seq_layout):
    w = WG_WINDOW
    g0 = pl.program_id(0) * tq
    (n_a, len_a), (_, len_b) = seq_layout
    in_a = g0 < n_a * len_a
    seq_len = jnp.where(in_a, jnp.int32(len_a), jnp.int32(len_b))
    pos0 = jnp.where(in_a, lax.rem(g0, jnp.int32(len_a)), lax.rem(g0 - n_a * len_a, jnp.int32(len_b)))

    win_ref[0:w, :] = kvp_ref[...]
    win_ref[w:w + tq, :] = kv_ref[...]
    win_ref[w + tq:w + tq + w, :] = kvn_ref[...]

    rows = group * w
    a_idx = lax.broadcasted_iota(jnp.int32, (rows, 3 * w), 0) & (w - 1)
    c_idx = lax.broadcasted_iota(jnp.int32, (rows, 3 * w), 1)
    rel = c_idx - a_idx
    band = (rel >= 0) & (rel <= 2 * w)
    head_in_group = lax.broadcasted_iota(jnp.int32, (rows, 1), 0) // w

    for n in range(tq // w):
        kpos = pos0 + (n - 1) * w + c_idx
        valid = band & (kpos >= 0) & (kpos < seq_len)
        for g in range(n_kv):
            kw = win_ref[n * w:n * w + 3 * w, g * hd:(g + 1) * hd]
            vw = win_ref[n * w:n * w + 3 * w, (n_kv + g) * hd:(n_kv + g + 1) * hd]
            qg = jnp.concatenate(
                [q_ref[n * w:(n + 1) * w, (g * group + j) * hd:(g * group + j + 1) * hd]
                 for j in range(group)], axis=0)
            s = lax.dot_general(qg, kw, (((1,), (1,)), ((), ())), preferred_element_type=F32)
            s = jnp.where(valid, s, -jnp.inf)
            sk = jnp.zeros((rows, 1), F32)
            for j in range(group):
                sk = jnp.where(head_in_group == j, sink_ref[g * group + j], sk)
            m = jnp.maximum(jnp.max(s, axis=-1, keepdims=True), sk)
            e = jnp.exp(s - m)
            den = jnp.sum(e, axis=-1, keepdims=True) + jnp.exp(sk - m)
            p = (e * (1.0 / den)).astype(BF16)
            o = jnp.dot(p, vw, preferred_element_type=F32)
            for j in range(group):
                h = g * group + j
                o_ref[n * w:(n + 1) * w, h * hd:(h + 1) * hd] = o[j * w:(j + 1) * w, :].astype(o_ref.dtype)


def _window_attn(q, kv, sink, *, n_heads, n_kv, tq, seq_layout):
    t, qd = q.shape
    hd = qd // n_heads
    w = WG_WINDOW
    r = tq // w
    n_blk = t // w
    kernel = functools.partial(_window_attn_kernel, tq=tq, n_kv=n_kv, group=n_heads // n_kv, hd=hd,
                               seq_layout=seq_layout)
    return pl.pallas_call(
        kernel,
        grid=(t // tq,),
        in_specs=[
            pl.BlockSpec(memory_space=pltpu.SMEM),
            pl.BlockSpec((tq, qd), lambda i: (i, 0)),
            pl.BlockSpec((tq, 2 * n_kv * hd), lambda i: (i, 0)),
            pl.BlockSpec((w, 2 * n_kv * hd), lambda i: (jnp.maximum(i * r - 1, 0), 0)),
            pl.BlockSpec((w, 2 * n_kv * hd), lambda i: (jnp.minimum((i + 1) * r, n_blk - 1), 0)),
        ],
        out_specs=pl.BlockSpec((tq, qd), lambda i: (i, 0)),
        scratch_shapes=[pltpu.VMEM((tq + 2 * w, 2 * n_kv * hd), BF16)],
        out_shape=jax.ShapeDtypeStruct((t, qd), BF16),
        compiler_params=_cparams(1),
        name="window_attn",
    )(sink, q, kv, kv, kv)


def _outproj_kernel(x_ref, o_ref, w_ref, g_ref, b_ref, y_ref, *, alpha):
    h = jnp.dot(o_ref[...], w_ref[...], preferred_element_type=F32)
    y_ref[...] = _layernorm_rows(alpha * x_ref[...] + h, g_ref[...], b_ref[...])


def _outproj(x, o, w, g, b, *, alpha, tm):
    t, d = x.shape
    k = o.shape[1]
    return pl.pallas_call(
        functools.partial(_outproj_kernel, alpha=alpha),
        grid=(t // tm,),
        in_specs=[
            pl.BlockSpec((tm, d), lambda i: (i, 0)),
            pl.BlockSpec((tm, k), lambda i: (i, 0)),
            pl.BlockSpec((k, d), lambda i: (0, 0)),
            pl.BlockSpec((1, d), lambda i: (0, 0)),
            pl.BlockSpec((1, d), lambda i: (0, 0)),
        ],
        out_specs=pl.BlockSpec((tm, d), lambda i: (i, 0)),
        out_shape=jax.ShapeDtypeStruct((t, d), F32),
        compiler_params=_cparams(1),
        name="outproj_ln",
    )(x, o, w, g, b)


def _largest_divisor(n, cap, multiple):
    best = None
    for c in range(multiple, min(n, cap) + 1, multiple):
        if n % c == 0:
            best = c
    assert best is not None, (n, cap, multiple)
    return best


def _tiles(seq_lens, d_ff):
    common = math.gcd(*seq_lens)
    return dict(
        tm=_largest_divisor(common, 512, LANES),
        tf=_largest_divisor(d_ff, 512, 2 * LANES),
        tq=_largest_divisor(common, 256, LANES),
        tk=_largest_divisor(common // 2, 512, LANES),
        tw=_largest_divisor(common, 512, WG_WINDOW),
    )


def kernel(x_prompt, x_sample, ffn1_w_gate, ffn1_w_up, ffn1_w_down, ffn2_w_gate, ffn2_w_up, ffn2_w_down, ln_g, ln_b, da_w_qkv, da_w_o, da_lambda_q1, da_lambda_k1, da_lambda_q2, da_lambda_k2, da_subln_g, wg_w_qkv, wg_w_o, wg_sink):
    bp, sp, d = x_prompt.shape
    bs, ss, _ = x_sample.shape
    depth = ffn1_w_gate.shape[0]
    d_ff = ffn1_w_gate.shape[2]
    alpha = (2.0 * depth) ** 0.25

    da_hd = da_lambda_q1.shape[1]
    da_dv = da_subln_g.shape[1]
    da_heads = d // da_dv
    da_qd = 2 * da_heads * da_hd
    wg_heads = wg_sink.shape[1]
    wg_hd = d // wg_heads
    wg_kv = (wg_w_qkv.shape[2] - d) // (2 * wg_hd)
    assert 2 * da_hd == LANES and da_dv == LANES and wg_hd == LANES
    assert (bp * sp) % ss == 0 and sp % ss == 0

    tl = _tiles((sp, ss), d_ff)
    tm, tf, tq, tk, tw = tl["tm"], tl["tf"], tl["tq"], tl["tk"], tl["tw"]
    t_prompt = bp * sp
    seq_layout = ((bp, sp), (bs, ss))
    cos_a, sin_a = _rope_tables(max(sp, ss), da_hd)
    cos_b, sin_b = _rope_tables(max(sp, ss), wg_hd)

    x = jnp.concatenate([x_prompt.reshape(t_prompt, d), x_sample.reshape(bs * ss, d)], axis=0)

    def ln(i, k):
        return ln_g[i, k][None, :], ln_b[i, k][None, :]

    def ffn(x, wg, wu, wd, g, b):
        return _ffn(x, wg.astype(BF16), wu.astype(BF16), wd.astype(BF16), g, b, alpha=alpha, tm=tm, tf=tf)

    for i in range(depth):
        x = ffn(x, ffn1_w_gate[i], ffn1_w_up[i], ffn1_w_down[i], *ln(i, 0))
        j = i // 2
        if i % 2 == 0:
            lam_init = 0.8 - 0.6 * math.exp(-0.3 * i)
            w = da_w_qkv[j]
            qk = _proj_rope(x, w[:, :2 * da_qd].astype(BF16), cos_a, sin_a, head_dim=da_hd,
                            n_scaled=da_qd // LANES, scale=da_hd ** -0.5 * math.log2(math.e), tm=tm,
                            tn=da_qd // 2,
                            seq_layout=seq_layout)
            vt = _proj_t(x, w[:, 2 * da_qd:].T.astype(BF16), tm=tm, tk=tk)
            lam_args = (da_lambda_q1[j][None, :], da_lambda_k1[j][None, :],
                        da_lambda_q2[j][None, :], da_lambda_k2[j][None, :], da_subln_g[j][:, None])
            o_p = _diff_attn(qk, vt, *lam_args, row0=0, n_seq=bp, seq_len=sp, n_heads=da_heads,
                             tq=tq, tk=tk, lam_init=lam_init, out_rows=t_prompt)
            o_s = _diff_attn(qk, vt, *lam_args, row0=t_prompt, n_seq=bs, seq_len=ss, n_heads=da_heads,
                             tq=tq, tk=tk, lam_init=lam_init, out_rows=bs * ss)
            o = jnp.concatenate([o_p, o_s], axis=0)
            w_o = da_w_o[j]
        else:
            w = wg_w_qkv[j].astype(BF16)
            q = _proj_rope(x, w[:, :d], cos_b, sin_b, head_dim=wg_hd, n_scaled=d // LANES,
                           scale=wg_hd ** -0.5, tm=tm, tn=d // 2, seq_layout=seq_layout)
            k = _proj_rope(x, w[:, d:d + wg_kv * wg_hd], cos_b, sin_b, head_dim=wg_hd, n_scaled=0,
                           scale=1.0, tm=tm, tn=wg_kv * wg_hd, seq_layout=seq_layout)
            v = _proj(x, w[:, d + wg_kv * wg_hd:], tm=tm)
            kv = jnp.concatenate([k, v], axis=1)
            o = _window_attn(q, kv, wg_sink[j], n_heads=wg_heads, n_kv=wg_kv, tq=tw, seq_layout=seq_layout)
            w_o = wg_w_o[j]
        x = _outproj(x, o, w_o.astype(BF16), *ln(i, 1), alpha=alpha, tm=tm)
        x = ffn(x, ffn2_w_gate[i], ffn2_w_up[i], ffn2_w_down[i], *ln(i, 2))

    y_prompt = x[:t_prompt].reshape(bp, sp, d)
    y_sample = x[t_prompt:].reshape(bs, ss, d)
    return (y_prompt, y_sample)
```

```python
import functools
import math

import jax
import jax.numpy as jnp
from jax import lax
from jax.experimental import pallas as pl
from jax.experimental.pallas import tpu as pltpu

F32 = jnp.float32
BF16 = jnp.bfloat16

LN_EPS = 1e-5
ROPE_THETA = 10000.0
WG_WINDOW = 128
LANES = 128
V7X_VMEM_LIMIT_BYTES = 60000 * 1024


def _cparams(n_axes, flags=None):
    return pltpu.CompilerParams(
        dimension_semantics=("arbitrary",) * n_axes,
        vmem_limit_bytes=V7X_VMEM_LIMIT_BYTES,
        flags=flags,
    )


def _layernorm_rows(y, g, b):
    mu = jnp.mean(y, axis=-1, keepdims=True)
    yc = y - mu
    var = jnp.mean(yc * yc, axis=-1, keepdims=True)
    return yc * lax.rsqrt(var + LN_EPS) * g + b


def _ffn_kernel(x_ref, wg_ref, wu_ref, wd_ref, g_ref, b_ref, o_ref, xb_ref, acc_ref, *, alpha, nf):
    f = pl.program_id(1)

    @pl.when(f == 0)
    def _():
        xb_ref[...] = x_ref[...].astype(BF16)

    xb = xb_ref[...]
    hg = jnp.dot(xb, wg_ref[...], preferred_element_type=F32)
    hu = jnp.dot(xb, wu_ref[...], preferred_element_type=F32)
    a = (hg * jax.nn.sigmoid(hg) * hu).astype(BF16)
    part = jnp.dot(a, wd_ref[...], preferred_element_type=F32)

    @pl.when(f == 0)
    def _():
        acc_ref[...] = part

    @pl.when(f > 0)
    def _():
        acc_ref[...] += part

    @pl.when(f == nf - 1)
    def _():
        y = alpha * x_ref[...] + 0.5 * acc_ref[...]
        o_ref[...] = _layernorm_rows(y, g_ref[...], b_ref[...])


def _ffn(x, wg, wu, wd, g, b, *, alpha, tm, tf):
    t, d = x.shape
    f = wg.shape[1]
    nf = f // tf
    return pl.pallas_call(
        functools.partial(_ffn_kernel, alpha=alpha, nf=nf),
        grid=(t // tm, nf),
        in_specs=[
            pl.BlockSpec((tm, d), lambda i, j: (i, 0)),
            pl.BlockSpec((d, tf), lambda i, j: (0, j)),
            pl.BlockSpec((d, tf), lambda i, j: (0, j)),
            pl.BlockSpec((tf, d), lambda i, j: (j, 0)),
            pl.BlockSpec((1, d), lambda i, j: (0, 0)),
            pl.BlockSpec((1, d), lambda i, j: (0, 0)),
        ],
        out_specs=pl.BlockSpec((tm, d), lambda i, j: (i, 0)),
        out_shape=jax.ShapeDtypeStruct((t, d), F32),
        scratch_shapes=[pltpu.VMEM((tm, d), BF16), pltpu.VMEM((tm, d), F32)],
        compiler_params=_cparams(2),
        name="ffn_ln",
    )(x, wg, wu, wd, g, b)


def _rope_tables(max_len, head_dim):
    inv = ROPE_THETA ** (-jnp.arange(0, head_dim, 2, dtype=F32) / head_dim)
    ang = jnp.arange(max_len, dtype=F32)[:, None] * inv[None, :]
    ang = jnp.concatenate([ang, ang], -1)
    cos = jnp.cos(ang)
    sin = jnp.sin(ang)
    half = head_dim // 2
    sign = jnp.where(jnp.arange(head_dim) < half, -1.0, 1.0).astype(F32)
    sin = sin * sign[None, :]
    reps = LANES // head_dim
    return jnp.tile(cos, (1, reps)), jnp.tile(sin, (1, reps))


def _rope_lane_tile(y, cos, sin_signed, head_dim):
    half = head_dim // 2
    lane = lax.broadcasted_iota(jnp.int32, y.shape, 1)
    lo = (lane & (head_dim - 1)) < half
    fwd = pltpu.roll(y, half, 1)
    bwd = pltpu.roll(y, LANES - half, 1)
    return y * cos + jnp.where(lo, bwd, fwd) * sin_signed


def _proj_rope_kernel(x_ref, w_ref, cos_ref, sin_ref, o_ref, *, head_dim, n_scaled, scale):
    y = jnp.dot(x_ref[...].astype(BF16), w_ref[...], preferred_element_type=F32)
    cos = cos_ref[...]
    sin = sin_ref[...]
    j = pl.program_id(1)
    tiles_per_block = y.shape[1] // LANES
    for c in range(tiles_per_block):
        r = _rope_lane_tile(y[:, c * LANES:(c + 1) * LANES], cos, sin, head_dim)
        tile_idx = j * tiles_per_block + c
        r = r * jnp.where(tile_idx < n_scaled, scale, 1.0).astype(F32)
        o_ref[:, c * LANES:(c + 1) * LANES] = r.astype(o_ref.dtype)


def _pos_block(i, tm, seq_layout):
    (n_a, len_a), (_, len_b) = seq_layout
    tiles_a = n_a * len_a // tm
    return jnp.where(i < tiles_a, i % (len_a // tm), (i - tiles_a) % (len_b // tm))


def _proj_rope(x, w, cos, sin, *, head_dim, n_scaled, scale, tm, tn, seq_layout):
    t, d = x.shape
    n = w.shape[1]
    pos_spec = pl.BlockSpec((tm, LANES), lambda i, j: (_pos_block(i, tm, seq_layout), 0))
    return pl.pallas_call(
        functools.partial(_proj_rope_kernel, head_dim=head_dim, n_scaled=n_scaled, scale=scale),
        grid=(t // tm, n // tn),
        in_specs=[
            pl.BlockSpec((tm, d), lambda i, j: (i, 0)),
            pl.BlockSpec((d, tn), lambda i, j: (0, j)),
            pos_spec,
            pos_spec,
        ],
        out_specs=pl.BlockSpec((tm, tn), lambda i, j: (i, j)),
        out_shape=jax.ShapeDtypeStruct((t, n), BF16),
        compiler_params=_cparams(2),
        name="proj_rope",
    )(x, w, cos, sin)


def _proj_kernel(x_ref, w_ref, o_ref):
    o_ref[...] = jnp.dot(x_ref[...].astype(BF16), w_ref[...],
                         preferred_element_type=F32).astype(o_ref.dtype)


def _proj(x, w, *, tm):
    t, d = x.shape
    n = w.shape[1]
    return pl.pallas_call(
        _proj_kernel,
        grid=(t // tm,),
        in_specs=[
            pl.BlockSpec((tm, d), lambda i: (i, 0)),
            pl.BlockSpec((d, n), lambda i: (0, 0)),
        ],
        out_specs=pl.BlockSpec((tm, n), lambda i: (i, 0)),
        out_shape=jax.ShapeDtypeStruct((t, n), BF16),
        compiler_params=_cparams(1),
        name="proj",
    )(x, w)


def _proj_t_kernel(x_ref, wt_ref, o_ref, *, tk):
    yt = lax.dot_general(wt_ref[...], x_ref[...].astype(BF16), (((1,), (1,)), ((), ())),
                         preferred_element_type=F32)
    for c in range(o_ref.shape[0]):
        o_ref[c] = yt[:, c * tk:(c + 1) * tk].astype(o_ref.dtype)


def _proj_t(x, wt, *, tm, tk):
    t, d = x.shape
    n = wt.shape[0]
    return pl.pallas_call(
        functools.partial(_proj_t_kernel, tk=tk),
        grid=(t // tm,),
        in_specs=[
            pl.BlockSpec((tm, d), lambda i: (i, 0)),
            pl.BlockSpec((n, d), lambda i: (0, 0)),
        ],
        out_specs=pl.BlockSpec((tm // tk, n, tk), lambda i: (i, 0, 0)),
        out_shape=jax.ShapeDtypeStruct((t // tk, n, tk), BF16),
        compiler_params=_cparams(1),
        name="proj_t",
    )(x, wt)


ONES_ROWS = 16
DA_CHUNKS_PER_BODY = 4
DA_BODY_UNROLL = 1


def _diff_attn_kernel(q_ref, k_ref, vt_ref, lq1_ref, lk1_ref, lq2_ref, lk2_ref, g_ref, o_ref,
                      qb_ref, s_ref, cmax_ref, p_ref, a_ref, m_ref, acc_ref,
                      *, tq, tk, nk, group, hd, dv, lam_init):
    q = q_ref[...]
    lane = lax.broadcasted_iota(jnp.int32, q.shape, 1)
    zero = jnp.zeros_like(q)
    qb_ref[0:tq, :] = jnp.where(lane < hd, q, zero)
    qb_ref[tq:2 * tq, :] = jnp.where(lane >= hd, q, zero)
    ones = jnp.ones((ONES_ROWS, tk), BF16)

    def scores(i, slot):
        off = pl.multiple_of(i * tk, tk)
        k = k_ref[pl.ds(off, tk), :]
        s = lax.dot_general(k, qb_ref[...], (((1,), (1,)), ((), ())),
                            preferred_element_type=F32)
        s_ref[slot] = s
        cmax_ref[slot] = jnp.max(s, axis=0, keepdims=True)

    def softmax(slot):
        for c in range(2):
            m_old = m_ref[c]
            m_new = jnp.maximum(m_old, cmax_ref[slot, :, c * tq:(c + 1) * tq])
            a_ref[slot, c] = jnp.exp2(m_old - m_new)
            p_ref[slot, c] = jnp.exp2(s_ref[slot, :, c * tq:(c + 1) * tq] - m_new).astype(BF16)
            m_ref[c] = m_new

    def values(i, slot):
        vt = jnp.concatenate([vt_ref[i], ones], axis=0)
        for c in range(2):
            acc_ref[c] = a_ref[slot, c] * acc_ref[c] + jnp.dot(vt, p_ref[slot, c],
                                                               preferred_element_type=F32)

    m_ref[...] = jnp.full(m_ref.shape, -jnp.inf, F32)
    acc_ref[...] = jnp.zeros(acc_ref.shape, F32)

    def body(j, with_values=True, with_scores=True):
        if with_values:
            for g in range(group):
                values((j - 1) * group + g, g)
        for g in range(group):
            softmax(g)
        if with_scores:
            for g in range(group):
                scores((j + 1) * group + g, g)

    n_body = nk // group
    for g in range(group):
        scores(g, g)
    body(0, with_values=False)

    def loop_body(j, carry):
        body(j)
        return carry

    lax.fori_loop(1, n_body - 1, loop_body, 0, unroll=DA_BODY_UNROLL)
    body(n_body - 1, with_scores=False)
    for g in range(group):
        values(nk - group + g, g)

    lam = (jnp.exp(jnp.sum(lq1_ref[...] * lk1_ref[...], axis=-1, keepdims=True))
           - jnp.exp(jnp.sum(lq2_ref[...] * lk2_ref[...], axis=-1, keepdims=True)) + lam_init)
    o = (acc_ref[0, 0:dv] * (1.0 / acc_ref[0, dv:dv + 1])
         - lam * (acc_ref[1, 0:dv] * (1.0 / acc_ref[1, dv:dv + 1])))
    ms = jnp.mean(o * o, axis=0, keepdims=True)
    o = o * lax.rsqrt(ms + LN_EPS) * g_ref[...] * (1.0 - lam_init)
    o_ref[...] = o.T.astype(o_ref.dtype)


def _diff_attn(qk, vt, lq1, lk1, lq2, lk2, g_col, *, row0, n_seq, seq_len, n_heads, tq, tk, lam_init,
               out_rows):
    dv = g_col.shape[0]
    hd = lq1.shape[1]
    nq = seq_len // tq
    nk = seq_len // tk
    group = min(DA_CHUNKS_PER_BODY, nk // 2)
    assert nk % group == 0 and nk // group >= 2, "the chunk loop is software-pipelined in groups"
    q_blk0 = row0 // tq
    s_blk0 = row0 // seq_len
    kernel = functools.partial(_diff_attn_kernel, tq=tq, tk=tk, nk=nk, group=group, hd=hd, dv=dv,
                               lam_init=lam_init)
    vec = pl.BlockSpec((1, hd), lambda b, h, i: (0, 0))
    return pl.pallas_call(
        kernel,
        grid=(n_seq, n_heads, nq),
        in_specs=[
            pl.BlockSpec((tq, 2 * hd), lambda b, h, i: (q_blk0 + b * nq + i, h)),
            pl.BlockSpec((seq_len, 2 * hd), lambda b, h, i: (s_blk0 + b, n_heads + h)),
            pl.BlockSpec((nk, dv, tk), lambda b, h, i: (s_blk0 + b, h, 0)),
            vec, vec, vec, vec,
            pl.BlockSpec((dv, 1), lambda b, h, i: (0, 0)),
        ],
        out_specs=pl.BlockSpec((tq, dv), lambda b, h, i: (b * nq + i, h)),
        out_shape=jax.ShapeDtypeStruct((out_rows, n_heads * dv), BF16),
        scratch_shapes=[
            pltpu.VMEM((2 * tq, 2 * hd), BF16),
            pltpu.VMEM((group, tk, 2 * tq), F32),
            pltpu.VMEM((group, 1, 2 * tq), F32),
            pltpu.VMEM((group, 2, tk, tq), BF16),
            pltpu.VMEM((group, 2, 1, tq), F32),
            pltpu.VMEM((2, 1, tq), F32),
            pltpu.VMEM((2, dv + ONES_ROWS, tq), F32),
        ],
        compiler_params=_cparams(3),
        name="diff_attn",
    )(qk, qk, vt, lq1, lk1, lq2, lk2, g_col)


def _window_attn_kernel(sink_ref, q_ref, kv_ref, kvp_ref, kvn_ref, o_ref, win_ref,
                        *, tq, n_kv, group, hd, seq_layout):
    w = WG_WINDOW
    g0 = pl.program_id(0) * tq
    (n_a, len_a), (_, len_b) = seq_layout
    in_a = g0 < n_a * len_a
    seq_len = jnp.where(in_a, jnp.int32(len_a), jnp.int32(len_b))
    pos0 = jnp.where(in_a, lax.rem(g0, jnp.int32(len_a)), lax.rem(g0 - n_a * len_a, jnp.int32(len_b)))

    win_ref[0:w, :] = kvp_ref[...]
    win_ref[w:w + tq, :] = kv_ref[...]
    win_ref[w + tq:w + tq + w, :] = kvn_ref[...]

    rows = group * w
    a_idx = lax.broadcasted_iota(jnp.int32, (rows, 3 * w), 0) & (w - 1)
    c_idx = lax.broadcasted_iota(jnp.int32, (rows, 3 * w), 1)
    rel = c_idx - a_idx
    band = (rel >= 0) & (rel <= 2 * w)
    head_in_group = lax.broadcasted_iota(jnp.int32, (rows, 1), 0) // w

    for n in range(tq // w):
        kpos = pos0 + (n - 1) * w + c_idx
        valid = band & (kpos >= 0) & (kpos < seq_len)
        for g in range(n_kv):
            kw = win_ref[n * w:n * w + 3 * w, g * hd:(g + 1) * hd]
            vw = win_ref[n * w:n * w + 3 * w, (n_kv + g) * hd:(n_kv + g + 1) * hd]
            qg = jnp.concatenate(
                [q_ref[n * w:(n + 1) * w, (g * group + j) * hd:(g * group + j + 1) * hd]
                 for j in range(group)], axis=0)
            s = lax.dot_general(qg, kw, (((1,), (1,)), ((), ())), preferred_element_type=F32)
            s = jnp.where(valid, s, -jnp.inf)
            sk = jnp.zeros((rows, 1), F32)
            for j in range(group):
                sk = jnp.where(head_in_group == j, sink_ref[g * group + j], sk)
            m = jnp.maximum(jnp.max(s, axis=-1, keepdims=True), sk)
            e = jnp.exp(s - m)
            den = jnp.sum(e, axis=-1, keepdims=True) + jnp.exp(sk - m)
            p = (e * (1.0 / den)).astype(BF16)
            o = jnp.dot(p, vw, preferred_element_type=F32)
            for j in range(group):
                h = g * group + j
                o_ref[n * w:(n + 1) * w, h * hd:(h + 1) * hd] = o[j * w:(j + 1) * w, :].astype(o_ref.dtype)


def _window_attn(q, kv, sink, *, n_heads, n_kv, tq, seq_layout):
    t, qd = q.shape
    hd = qd // n_heads
    w = WG_WINDOW
    r = tq // w
    n_blk = t // w
    kernel = functools.partial(_window_attn_kernel, tq=tq, n_kv=n_kv, group=n_heads // n_kv, hd=hd,
                               seq_layout=seq_layout)
    return pl.pallas_call(
        kernel,
        grid=(t // tq,),
        in_specs=[
            pl.BlockSpec(memory_space=pltpu.SMEM),
            pl.BlockSpec((tq, qd), lambda i: (i, 0)),
            pl.BlockSpec((tq, 2 * n_kv * hd), lambda i: (i, 0)),
            pl.BlockSpec((w, 2 * n_kv * hd), lambda i: (jnp.maximum(i * r - 1, 0), 0)),
            pl.BlockSpec((w, 2 * n_kv * hd), lambda i: (jnp.minimum((i + 1) * r, n_blk - 1), 0)),
        ],
        out_specs=pl.BlockSpec((tq, qd), lambda i: (i, 0)),
        scratch_shapes=[pltpu.VMEM((tq + 2 * w, 2 * n_kv * hd), BF16)],
        out_shape=jax.ShapeDtypeStruct((t, qd), BF16),
        compiler_params=_cparams(1),
        name="window_attn",
    )(sink, q, kv, kv, kv)


def _outproj_kernel(x_ref, o_ref, w_ref, g_ref, b_ref, y_ref, *, alpha):
    h = jnp.dot(o_ref[...], w_ref[...], preferred_element_type=F32)
    y_ref[...] = _layernorm_rows(alpha * x_ref[...] + h, g_ref[...], b_ref[...])


def _outproj(x, o, w, g, b, *, alpha, tm):
    t, d = x.shape
    k = o.shape[1]
    return pl.pallas_call(
        functools.partial(_outproj_kernel, alpha=alpha),
        grid=(t // tm,),
        in_specs=[
            pl.BlockSpec((tm, d), lambda i: (i, 0)),
            pl.BlockSpec((tm, k), lambda i: (i, 0)),
            pl.BlockSpec((k, d), lambda i: (0, 0)),
            pl.BlockSpec((1, d), lambda i: (0, 0)),
            pl.BlockSpec((1, d), lambda i: (0, 0)),
        ],
        out_specs=pl.BlockSpec((tm, d), lambda i: (i, 0)),
        out_shape=jax.ShapeDtypeStruct((t, d), F32),
        compiler_params=_cparams(1),
        name="outproj_ln",
    )(x, o, w, g, b)


def _largest_divisor(n, cap, multiple):
    best = None
    for c in range(multiple, min(n, cap) + 1, multiple):
        if n % c == 0:
            best = c
    assert best is not None, (n, cap, multiple)
    return best


def _tiles(seq_lens, d_ff):
    common = math.gcd(*seq_lens)
    return dict(
        tm=_largest_divisor(common, 512, LANES),
        tf=_largest_divisor(d_ff, 512, 2 * LANES),
        tq=_largest_divisor(common, 512, LANES),
        tk=_largest_divisor(common // 2, 512, LANES),
        tw=_largest_divisor(common, 512, WG_WINDOW),
    )


def kernel(x_prompt, x_sample, ffn1_w_gate, ffn1_w_up, ffn1_w_down, ffn2_w_gate, ffn2_w_up, ffn2_w_down, ln_g, ln_b, da_w_qkv, da_w_o, da_lambda_q1, da_lambda_k1, da_lambda_q2, da_lambda_k2, da_subln_g, wg_w_qkv, wg_w_o, wg_sink):
    bp, sp, d = x_prompt.shape
    bs, ss, _ = x_sample.shape
    depth = ffn1_w_gate.shape[0]
    d_ff = ffn1_w_gate.shape[2]
    alpha = (2.0 * depth) ** 0.25

    da_hd = da_lambda_q1.shape[1]
    da_dv = da_subln_g.shape[1]
    da_heads = d // da_dv
    da_qd = 2 * da_heads * da_hd
    wg_heads = wg_sink.shape[1]
    wg_hd = d // wg_heads
    wg_kv = (wg_w_qkv.shape[2] - d) // (2 * wg_hd)
    assert 2 * da_hd == LANES and da_dv == LANES and wg_hd == LANES
    assert (bp * sp) % ss == 0 and sp % ss == 0

    tl = _tiles((sp, ss), d_ff)
    tm, tf, tq, tk, tw = tl["tm"], tl["tf"], tl["tq"], tl["tk"], tl["tw"]
    t_prompt = bp * sp
    seq_layout = ((bp, sp), (bs, ss))
    cos_a, sin_a = _rope_tables(max(sp, ss), da_hd)
    cos_b, sin_b = _rope_tables(max(sp, ss), wg_hd)

    x = jnp.concatenate([x_prompt.reshape(t_prompt, d), x_sample.reshape(bs * ss, d)], axis=0)

    def ln(i, k):
        return ln_g[i, k][None, :], ln_b[i, k][None, :]

    def ffn(x, wg, wu, wd, g, b):
        return _ffn(x, wg.astype(BF16), wu.astype(BF16), wd.astype(BF16), g, b, alpha=alpha, tm=tm, tf=tf)

    for i in range(depth):
        x = ffn(x, ffn1_w_gate[i], ffn1_w_up[i], ffn1_w_down[i], *ln(i, 0))
        j = i // 2
        if i % 2 == 0:
            lam_init = 0.8 - 0.6 * math.exp(-0.3 * i)
            w = da_w_qkv[j]
            qk = _proj_rope(x, w[:, :2 * da_qd].astype(BF16), cos_a, sin_a, head_dim=da_hd,
                            n_scaled=da_qd // LANES, scale=da_hd ** -0.5 * math.log2(math.e), tm=tm,
                            tn=da_qd // 2,
                            seq_layout=seq_layout)
            vt = _proj_t(x, w[:, 2 * da_qd:].T.astype(BF16), tm=tm, tk=tk)
            lam_args = (da_lambda_q1[j][None, :], da_lambda_k1[j][None, :],
                        da_lambda_q2[j][None, :], da_lambda_k2[j][None, :], da_subln_g[j][:, None])
            o_p = _diff_attn(qk, vt, *lam_args, row0=0, n_seq=bp, seq_len=sp, n_heads=da_heads,
                             tq=tq, tk=tk, lam_init=lam_init, out_rows=t_prompt)
            o_s = _diff_attn(qk, vt, *lam_args, row0=t_prompt, n_seq=bs, seq_len=ss, n_heads=da_heads,
                             tq=tq, tk=tk, lam_init=lam_init, out_rows=bs * ss)
            o = jnp.concatenate([o_p, o_s], axis=0)
            w_o = da_w_o[j]
        else:
            w = wg_w_qkv[j].astype(BF16)
            q = _proj_rope(x, w[:, :d], cos_b, sin_b, head_dim=wg_hd, n_scaled=d // LANES,
                           scale=wg_hd ** -0.5, tm=tm, tn=d // 2, seq_layout=seq_layout)
            k = _proj_rope(x, w[:, d:d + wg_kv * wg_hd], cos_b, sin_b, head_dim=wg_hd, n_scaled=0,
                           scale=1.0, tm=tm, tn=wg_kv * wg_hd, seq_layout=seq_layout)
            v = _proj(x, w[:, d + wg_kv * wg_hd:], tm=tm)
            kv = jnp.concatenate([k, v], axis=1)
            o = _window_attn(q, kv, wg_sink[j], n_heads=wg_heads, n_kv=wg_kv, tq=tw, seq_layout=seq_layout)
            w_o = wg_w_o[j]
        x = _outproj(x, o, w_o.astype(BF16), *ln(i, 1), alpha=alpha, tm=tm)
        x = ffn(x, ffn2_w_gate[i], ffn2_w_up[i], ffn2_w_down[i], *ln(i, 2))

    y_prompt = x[:t_prompt].reshape(bp, sp, d)
    y_sample = x[t_prompt:].reshape(bs, ss, d)
    return (y_prompt, y_sample)
```

```python
import functools
import math

import jax
import jax.numpy as jnp
from jax import lax
from jax.experimental import pallas as pl
from jax.experimental.pallas import tpu as pltpu

F32 = jnp.float32
BF16 = jnp.bfloat16

LN_EPS = 1e-5
ROPE_THETA = 10000.0
WG_WINDOW = 128
LANES = 128
V7X_VMEM_LIMIT_BYTES = 60000 * 1024


def _cparams(n_axes):
    return pltpu.CompilerParams(
        dimension_semantics=("arbitrary",) * n_axes,
        vmem_limit_bytes=V7X_VMEM_LIMIT_BYTES,
    )


def _layernorm_rows(y, g, b):
    mu = jnp.mean(y, axis=-1, keepdims=True)
    yc = y - mu
    var = jnp.mean(yc * yc, axis=-1, keepdims=True)
    return yc * lax.rsqrt(var + LN_EPS) * g + b


def _ffn_kernel(x_ref, wg_ref, wu_ref, wd_ref, g_ref, b_ref, o_ref, xb_ref, *, alpha, nf):
    f = pl.program_id(1)

    @pl.when(f == 0)
    def _():
        xb_ref[...] = x_ref[...].astype(BF16)
        o_ref[...] = jnp.zeros(o_ref.shape, F32)

    xb = xb_ref[...]
    hg = jnp.dot(xb, wg_ref[...], preferred_element_type=F32)
    hu = jnp.dot(xb, wu_ref[...], preferred_element_type=F32)
    a = (hg * jax.nn.sigmoid(hg) * hu).astype(BF16)
    o_ref[...] += jnp.dot(a, wd_ref[...], preferred_element_type=F32)

    @pl.when(f == nf - 1)
    def _():
        y = alpha * x_ref[...] + 0.5 * o_ref[...]
        o_ref[...] = _layernorm_rows(y, g_ref[...], b_ref[...])


def _ffn(x, wg, wu, wd, g, b, *, alpha, tm, tf):
    t, d = x.shape
    f = wg.shape[1]
    nf = f // tf
    return pl.pallas_call(
        functools.partial(_ffn_kernel, alpha=alpha, nf=nf),
        grid=(t // tm, nf),
        in_specs=[
            pl.BlockSpec((tm, d), lambda i, j: (i, 0)),
            pl.BlockSpec((d, tf), lambda i, j: (0, j)),
            pl.BlockSpec((d, tf), lambda i, j: (0, j)),
            pl.BlockSpec((tf, d), lambda i, j: (j, 0)),
            pl.BlockSpec((1, d), lambda i, j: (0, 0)),
            pl.BlockSpec((1, d), lambda i, j: (0, 0)),
        ],
        out_specs=pl.BlockSpec((tm, d), lambda i, j: (i, 0)),
        out_shape=jax.ShapeDtypeStruct((t, d), F32),
        scratch_shapes=[pltpu.VMEM((tm, d), BF16)],
        compiler_params=_cparams(2),
        name="ffn_ln",
    )(x, wg, wu, wd, g, b)


def _rope_tables(max_len, head_dim):
    inv = ROPE_THETA ** (-jnp.arange(0, head_dim, 2, dtype=F32) / head_dim)
    ang = jnp.arange(max_len, dtype=F32)[:, None] * inv[None, :]
    ang = jnp.concatenate([ang, ang], -1)
    cos = jnp.cos(ang)
    sin = jnp.sin(ang)
    half = head_dim // 2
    sign = jnp.where(jnp.arange(head_dim) < half, -1.0, 1.0).astype(F32)
    sin = sin * sign[None, :]
    reps = LANES // head_dim
    return jnp.tile(cos, (1, reps)), jnp.tile(sin, (1, reps))


def _rope_lane_tile(y, cos, sin_signed, head_dim):
    half = head_dim // 2
    lane = lax.broadcasted_iota(jnp.int32, y.shape, 1)
    lo = (lane & (head_dim - 1)) < half
    fwd = pltpu.roll(y, half, 1)
    bwd = pltpu.roll(y, LANES - half, 1)
    return y * cos + jnp.where(lo, bwd, fwd) * sin_signed


def _proj_rope_kernel(x_ref, w_ref, cos_ref, sin_ref, o_ref, *, head_dim, n_scaled, scale):
    y = jnp.dot(x_ref[...].astype(BF16), w_ref[...], preferred_element_type=F32)
    cos = cos_ref[...]
    sin = sin_ref[...]
    j = pl.program_id(1)
    tiles_per_block = y.shape[1] // LANES
    for c in range(tiles_per_block):
        r = _rope_lane_tile(y[:, c * LANES:(c + 1) * LANES], cos, sin, head_dim)
        tile_idx = j * tiles_per_block + c
        r = r * jnp.where(tile_idx < n_scaled, scale, 1.0).astype(F32)
        o_ref[:, c * LANES:(c + 1) * LANES] = r.astype(o_ref.dtype)


def _proj_rope(x, w, cos, sin, *, head_dim, n_scaled, scale, tm, tn, seq_len):
    t, d = x.shape
    n = w.shape[1]
    pos_spec = pl.BlockSpec((tm, LANES), lambda i, j: (i % (seq_len // tm), 0))
    return pl.pallas_call(
        functools.partial(_proj_rope_kernel, head_dim=head_dim, n_scaled=n_scaled, scale=scale),
        grid=(t // tm, n // tn),
        in_specs=[
            pl.BlockSpec((tm, d), lambda i, j: (i, 0)),
            pl.BlockSpec((d, tn), lambda i, j: (0, j)),
            pos_spec,
            pos_spec,
        ],
        out_specs=pl.BlockSpec((tm, tn), lambda i, j: (i, j)),
        out_shape=jax.ShapeDtypeStruct((t, n), BF16),
        compiler_params=_cparams(2),
        name="proj_rope",
    )(x, w, cos, sin)


def _proj_kernel(x_ref, w_ref, o_ref):
    o_ref[...] = jnp.dot(x_ref[...].astype(BF16), w_ref[...],
                         preferred_element_type=F32).astype(o_ref.dtype)


def _proj(x, w, *, tm):
    t, d = x.shape
    n = w.shape[1]
    return pl.pallas_call(
        _proj_kernel,
        grid=(t // tm,),
        in_specs=[
            pl.BlockSpec((tm, d), lambda i: (i, 0)),
            pl.BlockSpec((d, n), lambda i: (0, 0)),
        ],
        out_specs=pl.BlockSpec((tm, n), lambda i: (i, 0)),
        out_shape=jax.ShapeDtypeStruct((t, n), BF16),
        compiler_params=_cparams(1),
        name="proj",
    )(x, w)


def _proj_t_kernel(x_ref, wt_ref, o_ref, *, tk):
    yt = lax.dot_general(wt_ref[...], x_ref[...].astype(BF16), (((1,), (1,)), ((), ())),
                         preferred_element_type=F32)
    for c in range(o_ref.shape[0]):
        o_ref[c] = yt[:, c * tk:(c + 1) * tk].astype(o_ref.dtype)


def _proj_t(x, wt, *, tm, tk):
    t, d = x.shape
    n = wt.shape[0]
    assert tm % tk == 0
    return pl.pallas_call(
        functools.partial(_proj_t_kernel, tk=tk),
        grid=(t // tm,),
        in_specs=[
            pl.BlockSpec((tm, d), lambda i: (i, 0)),
            pl.BlockSpec((n, d), lambda i: (0, 0)),
        ],
        out_specs=pl.BlockSpec((tm // tk, n, tk), lambda i: (i, 0, 0)),
        out_shape=jax.ShapeDtypeStruct((t // tk, n, tk), BF16),
        compiler_params=_cparams(1),
        name="proj_t",
    )(x, wt)


ONES_ROWS = 16
DA_CHUNKS_PER_BODY = 4
DA_BODY_UNROLL = 1


def _diff_attn_kernel(q_ref, k_ref, vt_ref, lq1_ref, lk1_ref, lq2_ref, lk2_ref, g_ref, o_ref,
                      qb_ref, s_ref, cmax_ref, p_ref, a_ref, m_ref, acc_ref,
                      *, tq, tk, nk, group, hd, dv, lam_init):
    q = q_ref[...]
    lane = lax.broadcasted_iota(jnp.int32, q.shape, 1)
    zero = jnp.zeros_like(q)
    qb_ref[0:tq, :] = jnp.where(lane < hd, q, zero)
    qb_ref[tq:2 * tq, :] = jnp.where(lane >= hd, q, zero)
    ones = jnp.ones((ONES_ROWS, tk), BF16)

    def scores(i, slot):
        off = pl.multiple_of(i * tk, tk)
        k = k_ref[pl.ds(off, tk), :]
        s = lax.dot_general(k, qb_ref[...], (((1,), (1,)), ((), ())),
                            preferred_element_type=F32)
        s_ref[slot] = s
        cmax_ref[slot] = jnp.max(s, axis=0, keepdims=True)

    def softmax(slot):
        for c in range(2):
            m_old = m_ref[c]
            m_new = jnp.maximum(m_old, cmax_ref[slot, :, c * tq:(c + 1) * tq])
            a_ref[slot, c] = jnp.exp2(m_old - m_new)
            p_ref[slot, c] = jnp.exp2(s_ref[slot, :, c * tq:(c + 1) * tq] - m_new).astype(BF16)
            m_ref[c] = m_new

    def values(i, slot):
        vt = jnp.concatenate([vt_ref[i], ones], axis=0)
        for c in range(2):
            acc_ref[c] = a_ref[slot, c] * acc_ref[c] + jnp.dot(vt, p_ref[slot, c],
                                                               preferred_element_type=F32)

    m_ref[...] = jnp.full(m_ref.shape, -jnp.inf, F32)
    acc_ref[...] = jnp.zeros(acc_ref.shape, F32)

    def body(j, with_values=True, with_scores=True):
        if with_values:
            for g in range(group):
                values((j - 1) * group + g, g)
        for g in range(group):
            softmax(g)
        if with_scores:
            for g in range(group):
                scores((j + 1) * group + g, g)

    n_body = nk // group
    for g in range(group):
        scores(g, g)
    body(0, with_values=False)

    def loop_body(j, carry):
        body(j)
        return carry

    lax.fori_loop(1, n_body - 1, loop_body, 0, unroll=DA_BODY_UNROLL)
    body(n_body - 1, with_scores=False)
    for g in range(group):
        values(nk - group + g, g)

    lam = (jnp.exp(jnp.sum(lq1_ref[...] * lk1_ref[...], axis=-1, keepdims=True))
           - jnp.exp(jnp.sum(lq2_ref[...] * lk2_ref[...], axis=-1, keepdims=True)) + lam_init)
    o = (acc_ref[0, 0:dv] * (1.0 / acc_ref[0, dv:dv + 1])
         - lam * (acc_ref[1, 0:dv] * (1.0 / acc_ref[1, dv:dv + 1])))
    ms = jnp.mean(o * o, axis=0, keepdims=True)
    o = o * lax.rsqrt(ms + LN_EPS) * g_ref[...] * (1.0 - lam_init)
    o_ref[...] = o.T.astype(o_ref.dtype)


def _diff_attn(qk, vt, lq1, lk1, lq2, lk2, g_col, *, seq_len, n_heads, tq, tk, lam_init):
    dv = g_col.shape[0]
    hd = lq1.shape[1]
    n_seq = qk.shape[0] // seq_len
    nq = seq_len // tq
    nk = seq_len // tk
    group = min(DA_CHUNKS_PER_BODY, nk // 2)
    assert nk % group == 0 and nk // group >= 2, "the chunk loop is software-pipelined in groups"
    kernel = functools.partial(_diff_attn_kernel, tq=tq, tk=tk, nk=nk, group=group, hd=hd, dv=dv,
                               lam_init=lam_init)
    vec = pl.BlockSpec((1, hd), lambda b, h, i: (0, 0))
    return pl.pallas_call(
        kernel,
        grid=(n_seq, n_heads, nq),
        in_specs=[
            pl.BlockSpec((tq, 2 * hd), lambda b, h, i: (b * nq + i, h)),
            pl.BlockSpec((seq_len, 2 * hd), lambda b, h, i: (b, n_heads + h)),
            pl.BlockSpec((nk, dv, tk), lambda b, h, i: (b, h, 0)),
            vec, vec, vec, vec,
            pl.BlockSpec((dv, 1), lambda b, h, i: (0, 0)),
        ],
        out_specs=pl.BlockSpec((tq, dv), lambda b, h, i: (b * nq + i, h)),
        out_shape=jax.ShapeDtypeStruct((qk.shape[0], n_heads * dv), BF16),
        scratch_shapes=[
            pltpu.VMEM((2 * tq, 2 * hd), BF16),
            pltpu.VMEM((group, tk, 2 * tq), F32),
            pltpu.VMEM((group, 1, 2 * tq), F32),
            pltpu.VMEM((group, 2, tk, tq), BF16),
            pltpu.VMEM((group, 2, 1, tq), F32),
            pltpu.VMEM((2, 1, tq), F32),
            pltpu.VMEM((2, dv + ONES_ROWS, tq), F32),
        ],
        compiler_params=_cparams(3),
        name="diff_attn",
    )(qk, qk, vt, lq1, lk1, lq2, lk2, g_col)


def _window_attn_kernel(sink_ref, q_ref, k_ref, kp_ref, kn_ref, v_ref, vp_ref, vn_ref, o_ref,
                        kwin_ref, vwin_ref, *, tq, n_kv, group, hd, seq_len):
    w = WG_WINDOW
    pos0 = lax.rem(pl.program_id(0) * tq, jnp.int32(seq_len))

    for win_ref, prev_ref, main_ref, next_ref in ((kwin_ref, kp_ref, k_ref, kn_ref),
                                                  (vwin_ref, vp_ref, v_ref, vn_ref)):
        win_ref[0:w, :] = prev_ref[...]
        win_ref[w:w + tq, :] = main_ref[...]
        win_ref[w + tq:w + tq + w, :] = next_ref[...]

    rows = group * w
    a_idx = lax.broadcasted_iota(jnp.int32, (rows, 3 * w), 0) & (w - 1)
    c_idx = lax.broadcasted_iota(jnp.int32, (rows, 3 * w), 1)
    rel = c_idx - a_idx
    band = (rel >= 0) & (rel <= 2 * w)
    head_in_group = lax.broadcasted_iota(jnp.int32, (rows, 1), 0) // w

    for n in range(tq // w):
        kpos = pos0 + (n - 1) * w + c_idx
        valid = band & (kpos >= 0) & (kpos < seq_len)
        for g in range(n_kv):
            kw = kwin_ref[n * w:n * w + 3 * w, g * hd:(g + 1) * hd]
            vw = vwin_ref[n * w:n * w + 3 * w, g * hd:(g + 1) * hd]
            qg = jnp.concatenate(
                [q_ref[n * w:(n + 1) * w, (g * group + j) * hd:(g * group + j + 1) * hd]
                 for j in range(group)], axis=0)
            s = lax.dot_general(qg, kw, (((1,), (1,)), ((), ())), preferred_element_type=F32)
            s = jnp.where(valid, s, -jnp.inf)
            sk = jnp.zeros((rows, 1), F32)
            for j in range(group):
                sk = jnp.where(head_in_group == j, sink_ref[g * group + j], sk)
            m = jnp.maximum(jnp.max(s, axis=-1, keepdims=True), sk)
            e = jnp.exp(s - m)
            den = jnp.sum(e, axis=-1, keepdims=True) + jnp.exp(sk - m)
            p = (e * (1.0 / den)).astype(BF16)
            o = jnp.dot(p, vw, preferred_element_type=F32)
            for j in range(group):
                h = g * group + j
                o_ref[n * w:(n + 1) * w, h * hd:(h + 1) * hd] = o[j * w:(j + 1) * w, :].astype(o_ref.dtype)


def _window_attn(q, k, v, sink, *, n_heads, n_kv, tq, seq_len):
    t, qd = q.shape
    hd = qd // n_heads
    kd = n_kv * hd
    w = WG_WINDOW
    r = tq // w
    n_blk = t // w
    kernel = functools.partial(_window_attn_kernel, tq=tq, n_kv=n_kv, group=n_heads // n_kv, hd=hd,
                               seq_len=seq_len)
    main = pl.BlockSpec((tq, kd), lambda i: (i, 0))
    prev = pl.BlockSpec((w, kd), lambda i: (jnp.maximum(i * r - 1, 0), 0))
    nxt = pl.BlockSpec((w, kd), lambda i: (jnp.minimum((i + 1) * r, n_blk - 1), 0))
    return pl.pallas_call(
        kernel,
        grid=(t // tq,),
        in_specs=[
            pl.BlockSpec(memory_space=pltpu.SMEM),
            pl.BlockSpec((tq, qd), lambda i: (i, 0)),
            main, prev, nxt,
            main, prev, nxt,
        ],
        out_specs=pl.BlockSpec((tq, qd), lambda i: (i, 0)),
        scratch_shapes=[pltpu.VMEM((tq + 2 * w, kd), BF16), pltpu.VMEM((tq + 2 * w, kd), BF16)],
        out_shape=jax.ShapeDtypeStruct((t, qd), BF16),
        compiler_params=_cparams(1),
        name="window_attn",
    )(sink, q, k, k, k, v, v, v)


def _outproj_kernel(x_ref, o_ref, w_ref, g_ref, b_ref, y_ref, *, alpha):
    h = jnp.dot(o_ref[...], w_ref[...], preferred_element_type=F32)
    y_ref[...] = _layernorm_rows(alpha * x_ref[...] + h, g_ref[...], b_ref[...])


def _outproj(x, o, w, g, b, *, alpha, tm):
    t, d = x.shape
    k = o.shape[1]
    return pl.pallas_call(
        functools.partial(_outproj_kernel, alpha=alpha),
        grid=(t // tm,),
        in_specs=[
            pl.BlockSpec((tm, d), lambda i: (i, 0)),
            pl.BlockSpec((tm, k), lambda i: (i, 0)),
            pl.BlockSpec((k, d), lambda i: (0, 0)),
            pl.BlockSpec((1, d), lambda i: (0, 0)),
            pl.BlockSpec((1, d), lambda i: (0, 0)),
        ],
        out_specs=pl.BlockSpec((tm, d), lambda i: (i, 0)),
        out_shape=jax.ShapeDtypeStruct((t, d), F32),
        compiler_params=_cparams(1),
        name="outproj_ln",
    )(x, o, w, g, b)


def _largest_divisor(n, cap, multiple):
    best = None
    for c in range(multiple, min(n, cap) + 1, multiple):
        if n % c == 0:
            best = c
    assert best is not None, (n, cap, multiple)
    return best


def _tiles(seq_len, d_ff):
    return dict(
        tm=_largest_divisor(seq_len, 512, LANES),
        tmf=_largest_divisor(seq_len, 512, LANES),
        tf=_largest_divisor(d_ff, 512, 2 * LANES),
        tq=_largest_divisor(seq_len, 512, LANES),
        tk=_largest_divisor(seq_len // 2, 512, LANES),
        tw=_largest_divisor(seq_len, 512, WG_WINDOW),
    )


def kernel(x_prompt, x_sample, ffn1_w_gate, ffn1_w_up, ffn1_w_down, ffn2_w_gate, ffn2_w_up, ffn2_w_down, ln_g, ln_b, da_w_qkv, da_w_o, da_lambda_q1, da_lambda_k1, da_lambda_q2, da_lambda_k2, da_subln_g, wg_w_qkv, wg_w_o, wg_sink):
    bp, sp, d = x_prompt.shape
    bs, ss, _ = x_sample.shape
    depth = ffn1_w_gate.shape[0]
    d_ff = ffn1_w_gate.shape[2]
    alpha = (2.0 * depth) ** 0.25

    da_hd = da_lambda_q1.shape[1]
    da_dv = da_subln_g.shape[1]
    da_heads = d // da_dv
    da_qd = 2 * da_heads * da_hd
    wg_heads = wg_sink.shape[1]
    wg_hd = d // wg_heads
    wg_kv = (wg_w_qkv.shape[2] - d) // (2 * wg_hd)
    assert 2 * da_hd == LANES and da_dv == LANES and wg_hd == LANES

    cos_a, sin_a = _rope_tables(max(sp, ss), da_hd)
    cos_b, sin_b = _rope_tables(max(sp, ss), wg_hd)

    ffn_w = [[(wg[i].astype(BF16), wu[i].astype(BF16), wd[i].astype(BF16))
              for wg, wu, wd in ((ffn1_w_gate, ffn1_w_up, ffn1_w_down), (ffn2_w_gate, ffn2_w_up, ffn2_w_down))]
             for i in range(depth)]
    mix_w = []
    for i in range(depth):
        j = i // 2
        if i % 2 == 0:
            w = da_w_qkv[j]
            mix_w.append((w[:, :2 * da_qd].astype(BF16), w[:, 2 * da_qd:].T.astype(BF16), da_w_o[j].astype(BF16)))
        else:
            w = wg_w_qkv[j].astype(BF16)
            kd = wg_kv * wg_hd
            mix_w.append((w[:, :d], w[:, d:d + kd], w[:, d + kd:], wg_w_o[j].astype(BF16)))

    def ln(i, k):
        return ln_g[i, k][None, :], ln_b[i, k][None, :]

    def trunk(x, seq_len):
        tl = _tiles(seq_len, d_ff)
        tm, tmf, tf, tq, tk, tw = tl["tm"], tl["tmf"], tl["tf"], tl["tq"], tl["tk"], tl["tw"]
        for i in range(depth):
            x = _ffn(x, *ffn_w[i][0], *ln(i, 0), alpha=alpha, tm=tmf, tf=tf)
            j = i // 2
            if i % 2 == 0:
                lam_init = 0.8 - 0.6 * math.exp(-0.3 * i)
                w_qk, w_vt, w_o = mix_w[i]
                qk = _proj_rope(x, w_qk, cos_a, sin_a, head_dim=da_hd, n_scaled=da_qd // LANES,
                                scale=da_hd ** -0.5 * math.log2(math.e), tm=tm, tn=da_qd // 2,
                                seq_len=seq_len)
                vt = _proj_t(x, w_vt, tm=tm, tk=tk)
                o = _diff_attn(qk, vt, da_lambda_q1[j][None, :], da_lambda_k1[j][None, :],
                               da_lambda_q2[j][None, :], da_lambda_k2[j][None, :], da_subln_g[j][:, None],
                               seq_len=seq_len, n_heads=da_heads, tq=tq, tk=tk, lam_init=lam_init)
            else:
                w_q, w_k, w_v, w_o = mix_w[i]
                q = _proj_rope(x, w_q, cos_b, sin_b, head_dim=wg_hd, n_scaled=d // LANES,
                               scale=wg_hd ** -0.5, tm=tm, tn=d // 2, seq_len=seq_len)
                k = _proj_rope(x, w_k, cos_b, sin_b, head_dim=wg_hd, n_scaled=0, scale=1.0, tm=tm,
                               tn=wg_kv * wg_hd, seq_len=seq_len)
                v = _proj(x, w_v, tm=tm)
                o = _window_attn(q, k, v, wg_sink[j], n_heads=wg_heads, n_kv=wg_kv, tq=tw, seq_len=seq_len)
            x = _outproj(x, o, w_o, *ln(i, 1), alpha=alpha, tm=tm)
            x = _ffn(x, *ffn_w[i][1], *ln(i, 2), alpha=alpha, tm=tmf, tf=tf)
        return x

    y_prompt = trunk(x_prompt.reshape(bp * sp, d), sp).reshape(bp, sp, d)
    y_sample = trunk(x_sample.reshape(bs * ss, d), ss).reshape(bs, ss, d)
    return (y_prompt, y_sample)
```

```python
import functools
import math

import jax
import jax.numpy as jnp
from jax import lax
from jax.experimental import pallas as pl
from jax.experimental.pallas import tpu as pltpu

F32 = jnp.float32
BF16 = jnp.bfloat16

LN_EPS = 1e-5
ROPE_THETA = 10000.0
WG_WINDOW = 128
LANES = 128
V7X_VMEM_LIMIT_BYTES = 60000 * 1024


def _cparams(n_axes):
    return pltpu.CompilerParams(
        dimension_semantics=("arbitrary",) * n_axes,
        vmem_limit_bytes=V7X_VMEM_LIMIT_BYTES,
    )


def _layernorm_rows(y, g, b):
    mu = jnp.mean(y, axis=-1, keepdims=True)
    yc = y - mu
    var = jnp.mean(yc * yc, axis=-1, keepdims=True)
    return yc * lax.rsqrt(var + LN_EPS) * g + b


FFN_ROW_SUBTILE = 512
LN_ROW_SUBTILE = 128


def _ffn_kernel(x_ref, wg_ref, wu_ref, wd_ref, g_ref, b_ref, o_ref, xb_ref, *, alpha, nf):
    f = pl.program_id(1)

    @pl.when(f == 0)
    def _():
        xb_ref[...] = x_ref[...].astype(BF16)
        o_ref[...] = jnp.zeros(o_ref.shape, F32)

    tr = min(FFN_ROW_SUBTILE, xb_ref.shape[0])
    for r in range(0, xb_ref.shape[0], tr):
        xb = xb_ref[r:r + tr, :]
        hg = jnp.dot(xb, wg_ref[...], preferred_element_type=F32)
        hu = jnp.dot(xb, wu_ref[...], preferred_element_type=F32)
        a = (hg * jax.nn.sigmoid(hg) * hu).astype(BF16)
        o_ref[r:r + tr, :] += jnp.dot(a, wd_ref[...], preferred_element_type=F32)

    @pl.when(f == nf - 1)
    def _():
        g = g_ref[...]
        b = b_ref[...]
        tl = min(LN_ROW_SUBTILE, o_ref.shape[0])
        for r in range(0, o_ref.shape[0], tl):
            y = alpha * x_ref[r:r + tl, :] + 0.5 * o_ref[r:r + tl, :]
            o_ref[r:r + tl, :] = _layernorm_rows(y, g, b)


def _ffn(x, wg, wu, wd, g, b, *, alpha, tm, tf):
    t, d = x.shape
    f = wg.shape[1]
    nf = f // tf
    return pl.pallas_call(
        functools.partial(_ffn_kernel, alpha=alpha, nf=nf),
        grid=(t // tm, nf),
        in_specs=[
            pl.BlockSpec((tm, d), lambda i, j: (i, 0)),
            pl.BlockSpec((d, tf), lambda i, j: (0, j)),
            pl.BlockSpec((d, tf), lambda i, j: (0, j)),
            pl.BlockSpec((tf, d), lambda i, j: (j, 0)),
            pl.BlockSpec((1, d), lambda i, j: (0, 0)),
            pl.BlockSpec((1, d), lambda i, j: (0, 0)),
        ],
        out_specs=pl.BlockSpec((tm, d), lambda i, j: (i, 0)),
        out_shape=jax.ShapeDtypeStruct((t, d), F32),
        scratch_shapes=[pltpu.VMEM((tm, d), BF16)],
        compiler_params=_cparams(2),
        name="ffn_ln",
    )(x, wg, wu, wd, g, b)


def _rope_tables(max_len, head_dim):
    inv = ROPE_THETA ** (-jnp.arange(0, head_dim, 2, dtype=F32) / head_dim)
    ang = jnp.arange(max_len, dtype=F32)[:, None] * inv[None, :]
    ang = jnp.concatenate([ang, ang], -1)
    cos = jnp.cos(ang)
    sin = jnp.sin(ang)
    half = head_dim // 2
    sign = jnp.where(jnp.arange(head_dim) < half, -1.0, 1.0).astype(F32)
    sin = sin * sign[None, :]
    reps = LANES // head_dim
    return jnp.tile(cos, (1, reps)), jnp.tile(sin, (1, reps))


def _rope_lane_tile(y, cos, sin_signed, head_dim):
    half = head_dim // 2
    lane = lax.broadcasted_iota(jnp.int32, y.shape, 1)
    lo = (lane & (head_dim - 1)) < half
    fwd = pltpu.roll(y, half, 1)
    bwd = pltpu.roll(y, LANES - half, 1)
    return y * cos + jnp.where(lo, bwd, fwd) * sin_signed


def _proj_rope_kernel(x_ref, w_ref, cos_ref, sin_ref, o_ref, *, head_dim, n_scaled, scale):
    y = jnp.dot(x_ref[...].astype(BF16), w_ref[...], preferred_element_type=F32)
    cos = cos_ref[...]
    sin = sin_ref[...]
    j = pl.program_id(1)
    tiles_per_block = y.shape[1] // LANES
    for c in range(tiles_per_block):
        r = _rope_lane_tile(y[:, c * LANES:(c + 1) * LANES], cos, sin, head_dim)
        tile_idx = j * tiles_per_block + c
        r = r * jnp.where(tile_idx < n_scaled, scale, 1.0).astype(F32)
        o_ref[:, c * LANES:(c + 1) * LANES] = r.astype(o_ref.dtype)


def _proj_rope(x, w, cos, sin, *, head_dim, n_scaled, scale, tm, tn, seq_len):
    t, d = x.shape
    n = w.shape[1]
    pos_spec = pl.BlockSpec((tm, LANES), lambda i, j: (i % (seq_len // tm), 0))
    return pl.pallas_call(
        functools.partial(_proj_rope_kernel, head_dim=head_dim, n_scaled=n_scaled, scale=scale),
        grid=(t // tm, n // tn),
        in_specs=[
            pl.BlockSpec((tm, d), lambda i, j: (i, 0)),
            pl.BlockSpec((d, tn), lambda i, j: (0, j)),
            pos_spec,
            pos_spec,
        ],
        out_specs=pl.BlockSpec((tm, tn), lambda i, j: (i, j)),
        out_shape=jax.ShapeDtypeStruct((t, n), BF16),
        compiler_params=_cparams(2),
        name="proj_rope",
    )(x, w, cos, sin)


def _proj_kernel(x_ref, w_ref, o_ref):
    o_ref[...] = jnp.dot(x_ref[...].astype(BF16), w_ref[...],
                         preferred_element_type=F32).astype(o_ref.dtype)


def _proj(x, w, *, tm):
    t, d = x.shape
    n = w.shape[1]
    return pl.pallas_call(
        _proj_kernel,
        grid=(t // tm,),
        in_specs=[
            pl.BlockSpec((tm, d), lambda i: (i, 0)),
            pl.BlockSpec((d, n), lambda i: (0, 0)),
        ],
        out_specs=pl.BlockSpec((tm, n), lambda i: (i, 0)),
        out_shape=jax.ShapeDtypeStruct((t, n), BF16),
        compiler_params=_cparams(1),
        name="proj",
    )(x, w)


def _proj_t_kernel(x_ref, wt_ref, o_ref, *, tk):
    yt = lax.dot_general(wt_ref[...], x_ref[...].astype(BF16), (((1,), (1,)), ((), ())),
                         preferred_element_type=F32)
    for c in range(o_ref.shape[0]):
        o_ref[c] = yt[:, c * tk:(c + 1) * tk].astype(o_ref.dtype)


def _proj_t(x, wt, *, tm, tk):
    t, d = x.shape
    n = wt.shape[0]
    assert tm % tk == 0
    return pl.pallas_call(
        functools.partial(_proj_t_kernel, tk=tk),
        grid=(t // tm,),
        in_specs=[
            pl.BlockSpec((tm, d), lambda i: (i, 0)),
            pl.BlockSpec((n, d), lambda i: (0, 0)),
        ],
        out_specs=pl.BlockSpec((tm // tk, n, tk), lambda i: (i, 0, 0)),
        out_shape=jax.ShapeDtypeStruct((t // tk, n, tk), BF16),
        compiler_params=_cparams(1),
        name="proj_t",
    )(x, wt)


ONES_ROWS = 16
DA_CHUNKS_PER_BODY = 4
DA_BODY_UNROLL = 1


def _diff_attn_kernel(q_ref, k_ref, vt_ref, lq1_ref, lk1_ref, lq2_ref, lk2_ref, g_ref, o_ref,
                      qb_ref, s_ref, cmax_ref, p_ref, a_ref, m_ref, acc_ref,
                      *, tq, tk, nk, group, hd, dv, lam_init):
    q = q_ref[...]
    lane = lax.broadcasted_iota(jnp.int32, q.shape, 1)
    zero = jnp.zeros_like(q)
    qb_ref[0:tq, :] = jnp.where(lane < hd, q, zero)
    qb_ref[tq:2 * tq, :] = jnp.where(lane >= hd, q, zero)
    ones = jnp.ones((ONES_ROWS, tk), BF16)

    def scores(i, slot):
        off = pl.multiple_of(i * tk, tk)
        k = k_ref[pl.ds(off, tk), :]
        s = lax.dot_general(k, qb_ref[...], (((1,), (1,)), ((), ())),
                            preferred_element_type=F32)
        s_ref[slot] = s
        cmax_ref[slot] = jnp.max(s, axis=0, keepdims=True)

    def softmax(slot):
        for c in range(2):
            m_old = m_ref[c]
            m_new = jnp.maximum(m_old, cmax_ref[slot, :, c * tq:(c + 1) * tq])
            a_ref[slot, c] = jnp.exp2(m_old - m_new)
            p_ref[slot, c] = jnp.exp2(s_ref[slot, :, c * tq:(c + 1) * tq] - m_new).astype(BF16)
            m_ref[c] = m_new

    def values(i, slot):
        vt = jnp.concatenate([vt_ref[i], ones], axis=0)
        for c in range(2):
            acc_ref[c] = a_ref[slot, c] * acc_ref[c] + jnp.dot(vt, p_ref[slot, c],
                                                               preferred_element_type=F32)

    m_ref[...] = jnp.full(m_ref.shape, -jnp.inf, F32)
    acc_ref[...] = jnp.zeros(acc_ref.shape, F32)

    def body(j, with_values=True, with_scores=True):
        if with_values:
            for g in range(group):
                values((j - 1) * group + g, g)
        for g in range(group):
            softmax(g)
        if with_scores:
            for g in range(group):
                scores((j + 1) * group + g, g)

    n_body = nk // group
    for g in range(group):
        scores(g, g)
    body(0, with_values=False)

    def loop_body(j, carry):
        body(j)
        return carry

    lax.fori_loop(1, n_body - 1, loop_body, 0, unroll=DA_BODY_UNROLL)
    body(n_body - 1, with_scores=False)
    for g in range(group):
        values(nk - group + g, g)

    lam = (jnp.exp(jnp.sum(lq1_ref[...] * lk1_ref[...], axis=-1, keepdims=True))
           - jnp.exp(jnp.sum(lq2_ref[...] * lk2_ref[...], axis=-1, keepdims=True)) + lam_init)
    o = (acc_ref[0, 0:dv] * (1.0 / acc_ref[0, dv:dv + 1])
         - lam * (acc_ref[1, 0:dv] * (1.0 / acc_ref[1, dv:dv + 1])))
    ms = jnp.mean(o * o, axis=0, keepdims=True)
    o = o * lax.rsqrt(ms + LN_EPS) * g_ref[...] * (1.0 - lam_init)
    o_ref[...] = o.T.astype(o_ref.dtype)


def _diff_attn(qk, vt, lq1, lk1, lq2, lk2, g_col, *, seq_len, n_heads, tq, tk, lam_init):
    dv = g_col.shape[0]
    hd = lq1.shape[1]
    n_seq = qk.shape[0] // seq_len
    nq = seq_len // tq
    nk = seq_len // tk
    group = min(DA_CHUNKS_PER_BODY, nk // 2)
    assert nk % group == 0 and nk // group >= 2, "the chunk loop is software-pipelined in groups"
    kernel = functools.partial(_diff_attn_kernel, tq=tq, tk=tk, nk=nk, group=group, hd=hd, dv=dv,
                               lam_init=lam_init)
    vec = pl.BlockSpec((1, hd), lambda b, h, i: (0, 0))
    return pl.pallas_call(
        kernel,
        grid=(n_seq, n_heads, nq),
        in_specs=[
            pl.BlockSpec((tq, 2 * hd), lambda b, h, i: (b * nq + i, h)),
            pl.BlockSpec((seq_len, 2 * hd), lambda b, h, i: (b, n_heads + h)),
            pl.BlockSpec((nk, dv, tk), lambda b, h, i: (b, h, 0)),
            vec, vec, vec, vec,
            pl.BlockSpec((dv, 1), lambda b, h, i: (0, 0)),
        ],
        out_specs=pl.BlockSpec((tq, dv), lambda b, h, i: (b * nq + i, h)),
        out_shape=jax.ShapeDtypeStruct((qk.shape[0], n_heads * dv), BF16),
        scratch_shapes=[
            pltpu.VMEM((2 * tq, 2 * hd), BF16),
            pltpu.VMEM((group, tk, 2 * tq), F32),
            pltpu.VMEM((group, 1, 2 * tq), F32),
            pltpu.VMEM((group, 2, tk, tq), BF16),
            pltpu.VMEM((group, 2, 1, tq), F32),
            pltpu.VMEM((2, 1, tq), F32),
            pltpu.VMEM((2, dv + ONES_ROWS, tq), F32),
        ],
        compiler_params=_cparams(3),
        name="diff_attn",
    )(qk, qk, vt, lq1, lk1, lq2, lk2, g_col)


def _window_attn_kernel(sink_ref, q_ref, k_ref, kp_ref, kn_ref, v_ref, vp_ref, vn_ref, o_ref,
                        kwin_ref, vwin_ref, *, tq, n_kv, group, hd, seq_len):
    w = WG_WINDOW
    pos0 = lax.rem(pl.program_id(0) * tq, jnp.int32(seq_len))

    for win_ref, prev_ref, main_ref, next_ref in ((kwin_ref, kp_ref, k_ref, kn_ref),
                                                  (vwin_ref, vp_ref, v_ref, vn_ref)):
        win_ref[0:w, :] = prev_ref[...]
        win_ref[w:w + tq, :] = main_ref[...]
        win_ref[w + tq:w + tq + w, :] = next_ref[...]

    rows = group * w
    a_idx = lax.broadcasted_iota(jnp.int32, (rows, 3 * w), 0) & (w - 1)
    c_idx = lax.broadcasted_iota(jnp.int32, (rows, 3 * w), 1)
    rel = c_idx - a_idx
    band = (rel >= 0) & (rel <= 2 * w)
    head_in_group = lax.broadcasted_iota(jnp.int32, (rows, 1), 0) // w

    for n in range(tq // w):
        kpos = pos0 + (n - 1) * w + c_idx
        valid = band & (kpos >= 0) & (kpos < seq_len)
        for g in range(n_kv):
            kw = kwin_ref[n * w:n * w + 3 * w, g * hd:(g + 1) * hd]
            vw = vwin_ref[n * w:n * w + 3 * w, g * hd:(g + 1) * hd]
            qg = jnp.concatenate(
                [q_ref[n * w:(n + 1) * w, (g * group + j) * hd:(g * group + j + 1) * hd]
                 for j in range(group)], axis=0)
            s = lax.dot_general(qg, kw, (((1,), (1,)), ((), ())), preferred_element_type=F32)
            s = jnp.where(valid, s, -jnp.inf)
            sk = jnp.zeros((rows, 1), F32)
            for j in range(group):
                sk = jnp.where(head_in_group == j, sink_ref[g * group + j], sk)
            m = jnp.maximum(jnp.max(s, axis=-1, keepdims=True), sk)
            e = jnp.exp(s - m)
            den = jnp.sum(e, axis=-1, keepdims=True) + jnp.exp(sk - m)
            p = (e * (1.0 / den)).astype(BF16)
            o = jnp.dot(p, vw, preferred_element_type=F32)
            for j in range(group):
                h = g * group + j
                o_ref[n * w:(n + 1) * w, h * hd:(h + 1) * hd] = o[j * w:(j + 1) * w, :].astype(o_ref.dtype)


def _window_attn(q, k, v, sink, *, n_heads, n_kv, tq, seq_len):
    t, qd = q.shape
    hd = qd // n_heads
    kd = n_kv * hd
    w = WG_WINDOW
    r = tq // w
    n_blk = t // w
    kernel = functools.partial(_window_attn_kernel, tq=tq, n_kv=n_kv, group=n_heads // n_kv, hd=hd,
                               seq_len=seq_len)
    main = pl.BlockSpec((tq, kd), lambda i: (i, 0))
    prev = pl.BlockSpec((w, kd), lambda i: (jnp.maximum(i * r - 1, 0), 0))
    nxt = pl.BlockSpec((w, kd), lambda i: (jnp.minimum((i + 1) * r, n_blk - 1), 0))
    return pl.pallas_call(
        kernel,
        grid=(t // tq,),
        in_specs=[
            pl.BlockSpec(memory_space=pltpu.SMEM),
            pl.BlockSpec((tq, qd), lambda i: (i, 0)),
            main, prev, nxt,
            main, prev, nxt,
        ],
        out_specs=pl.BlockSpec((tq, qd), lambda i: (i, 0)),
        scratch_shapes=[pltpu.VMEM((tq + 2 * w, kd), BF16), pltpu.VMEM((tq + 2 * w, kd), BF16)],
        out_shape=jax.ShapeDtypeStruct((t, qd), BF16),
        compiler_params=_cparams(1),
        name="window_attn",
    )(sink, q, k, k, k, v, v, v)


def _outproj_kernel(x_ref, o_ref, w_ref, g_ref, b_ref, y_ref, *, alpha):
    h = jnp.dot(o_ref[...], w_ref[...], preferred_element_type=F32)
    y_ref[...] = _layernorm_rows(alpha * x_ref[...] + h, g_ref[...], b_ref[...])


def _outproj(x, o, w, g, b, *, alpha, tm):
    t, d = x.shape
    k = o.shape[1]
    return pl.pallas_call(
        functools.partial(_outproj_kernel, alpha=alpha),
        grid=(t // tm,),
        in_specs=[
            pl.BlockSpec((tm, d), lambda i: (i, 0)),
            pl.BlockSpec((tm, k), lambda i: (i, 0)),
            pl.BlockSpec((k, d), lambda i: (0, 0)),
            pl.BlockSpec((1, d), lambda i: (0, 0)),
            pl.BlockSpec((1, d), lambda i: (0, 0)),
        ],
        out_specs=pl.BlockSpec((tm, d), lambda i: (i, 0)),
        out_shape=jax.ShapeDtypeStruct((t, d), F32),
        compiler_params=_cparams(1),
        name="outproj_ln",
    )(x, o, w, g, b)


def _largest_divisor(n, cap, multiple):
    best = None
    for c in range(multiple, min(n, cap) + 1, multiple):
        if n % c == 0:
            best = c
    assert best is not None, (n, cap, multiple)
    return best


def _tiles(seq_len, d_ff):
    return dict(
        tm=_largest_divisor(seq_len, 512, LANES),
        tmf=_largest_divisor(seq_len, 1024, LANES),
        tf=_largest_divisor(d_ff, 512, 2 * LANES),
        tq=_largest_divisor(seq_len, 1024, LANES),
        tk=_largest_divisor(seq_len // 2, 512, LANES),
        tw=_largest_divisor(seq_len, 512, WG_WINDOW),
    )


def kernel(x_prompt, x_sample, ffn1_w_gate, ffn1_w_up, ffn1_w_down, ffn2_w_gate, ffn2_w_up, ffn2_w_down, ln_g, ln_b, da_w_qkv, da_w_o, da_lambda_q1, da_lambda_k1, da_lambda_q2, da_lambda_k2, da_subln_g, wg_w_qkv, wg_w_o, wg_sink):
    bp, sp, d = x_prompt.shape
    bs, ss, _ = x_sample.shape
    depth = ffn1_w_gate.shape[0]
    d_ff = ffn1_w_gate.shape[2]
    alpha = (2.0 * depth) ** 0.25

    da_hd = da_lambda_q1.shape[1]
    da_dv = da_subln_g.shape[1]
    da_heads = d // da_dv
    da_qd = 2 * da_heads * da_hd
    wg_heads = wg_sink.shape[1]
    wg_hd = d // wg_heads
    wg_kv = (wg_w_qkv.shape[2] - d) // (2 * wg_hd)
    assert 2 * da_hd == LANES and da_dv == LANES and wg_hd == LANES

    cos_a, sin_a = _rope_tables(max(sp, ss), da_hd)
    cos_b, sin_b = _rope_tables(max(sp, ss), wg_hd)

    ffn_w = [[(wg[i].astype(BF16), wu[i].astype(BF16), wd[i].astype(BF16))
              for wg, wu, wd in ((ffn1_w_gate, ffn1_w_up, ffn1_w_down), (ffn2_w_gate, ffn2_w_up, ffn2_w_down))]
             for i in range(depth)]
    mix_w = []
    for i in range(depth):
        j = i // 2
        if i % 2 == 0:
            w = da_w_qkv[j]
            mix_w.append((w[:, :2 * da_qd].astype(BF16), w[:, 2 * da_qd:].T.astype(BF16), da_w_o[j].astype(BF16)))
        else:
            w = wg_w_qkv[j].astype(BF16)
            kd = wg_kv * wg_hd
            mix_w.append((w[:, :d], w[:, d:d + kd], w[:, d + kd:], wg_w_o[j].astype(BF16)))

    def ln(i, k):
        return ln_g[i, k][None, :], ln_b[i, k][None, :]

    def trunk(x, seq_len):
        tl = _tiles(seq_len, d_ff)
        tm, tmf, tf, tq, tk, tw = tl["tm"], tl["tmf"], tl["tf"], tl["tq"], tl["tk"], tl["tw"]
        for i in range(depth):
            x = _ffn(x, *ffn_w[i][0], *ln(i, 0), alpha=alpha, tm=tmf, tf=tf)
            j = i // 2
            if i % 2 == 0:
                lam_init = 0.8 - 0.6 * math.exp(-0.3 * i)
                w_qk, w_vt, w_o = mix_w[i]
                qk = _proj_rope(x, w_qk, cos_a, sin_a, head_dim=da_hd, n_scaled=da_qd // LANES,
                                scale=da_hd ** -0.5 * math.log2(math.e), tm=tm, tn=da_qd // 2,
                                seq_len=seq_len)
                vt = _proj_t(x, w_vt, tm=tm, tk=tk)
                o = _diff_attn(qk, vt, da_lambda_q1[j][None, :], da_lambda_k1[j][None, :],
                               da_lambda_q2[j][None, :], da_lambda_k2[j][None, :], da_subln_g[j][:, None],
                               seq_len=seq_len, n_heads=da_heads, tq=tq, tk=tk, lam_init=lam_init)
            else:
                w_q, w_k, w_v, w_o = mix_w[i]
                q = _proj_rope(x, w_q, cos_b, sin_b, head_dim=wg_hd, n_scaled=d // LANES,
                               scale=wg_hd ** -0.5, tm=tm, tn=d // 2, seq_len=seq_len)
                k = _proj_rope(x, w_k, cos_b, sin_b, head_dim=wg_hd, n_scaled=0, scale=1.0, tm=tm,
                               tn=wg_kv * wg_hd, seq_len=seq_len)
                v = _proj(x, w_v, tm=tm)
                o = _window_attn(q, k, v, wg_sink[j], n_heads=wg_heads, n_kv=wg_kv, tq=tw, seq_len=seq_len)
            x = _outproj(x, o, w_o, *ln(i, 1), alpha=alpha, tm=tm)
            x = _ffn(x, *ffn_w[i][1], *ln(i, 2), alpha=alpha, tm=tmf, tf=tf)
        return x

    y_prompt = trunk(x_prompt.reshape(bp * sp, d), sp).reshape(bp, sp, d)
    y_sample = trunk(x_sample.reshape(bs * ss, d), ss).reshape(bs, ss, d)
    return (y_prompt, y_sample)
```

```python
import functools
import math

import jax
import jax.numpy as jnp
from jax import lax
from jax.experimental import pallas as pl
from jax.experimental.pallas import tpu as pltpu

F32 = jnp.float32
BF16 = jnp.bfloat16

LN_EPS = 1e-5
ROPE_THETA = 10000.0
WG_WINDOW = 128
LANES = 128
V7X_VMEM_LIMIT_BYTES = 60000 * 1024


def _cparams(n_axes):
    return pltpu.CompilerParams(
        dimension_semantics=("arbitrary",) * n_axes,
        vmem_limit_bytes=V7X_VMEM_LIMIT_BYTES,
    )


def _layernorm_rows(y, g, b):
    mu = jnp.mean(y, axis=-1, keepdims=True)
    yc = y - mu
    var = jnp.mean(yc * yc, axis=-1, keepdims=True)
    return yc * lax.rsqrt(var + LN_EPS) * g + b


FFN_ROW_SUBTILE = 512
LN_ROW_SUBTILE = 128


def _ffn_kernel(x_ref, wg_ref, wu_ref, wd_ref, g_ref, b_ref, o_ref, xb_ref, *, alpha, nf):
    f = pl.program_id(1)
    tm = xb_ref.shape[0]
    tr = min(FFN_ROW_SUBTILE, tm)
    tl = min(LN_ROW_SUBTILE, tr)

    def step(first, last):
        for r in range(0, tm, tr):
            if first:
                xb_ref[r:r + tr, :] = x_ref[r:r + tr, :].astype(BF16)
            xb = xb_ref[r:r + tr, :]
            hg = jnp.dot(xb, wg_ref[...], preferred_element_type=F32)
            hu = jnp.dot(xb, wu_ref[...], preferred_element_type=F32)
            a = (hg * jax.nn.sigmoid(hg) * hu).astype(BF16)
            part = jnp.dot(a, wd_ref[...], preferred_element_type=F32)
            if first:
                o_ref[r:r + tr, :] = part
            else:
                o_ref[r:r + tr, :] += part
            if last:
                for q in range(r, r + tr, tl):
                    y = alpha * x_ref[q:q + tl, :] + 0.5 * o_ref[q:q + tl, :]
                    o_ref[q:q + tl, :] = _layernorm_rows(y, g_ref[...], b_ref[...])

    if nf == 1:
        step(True, True)
    else:
        pl.when(f == 0)(lambda: step(True, False))
        pl.when((f > 0) & (f < nf - 1))(lambda: step(False, False))
        pl.when(f == nf - 1)(lambda: step(False, True))


def _ffn(x, wg, wu, wd, g, b, *, alpha, tm, tf):
    t, d = x.shape
    f = wg.shape[1]
    nf = f // tf
    return pl.pallas_call(
        functools.partial(_ffn_kernel, alpha=alpha, nf=nf),
        grid=(t // tm, nf),
        in_specs=[
            pl.BlockSpec((tm, d), lambda i, j: (i, 0)),
            pl.BlockSpec((d, tf), lambda i, j: (0, j)),
            pl.BlockSpec((d, tf), lambda i, j: (0, j)),
            pl.BlockSpec((tf, d), lambda i, j: (j, 0)),
            pl.BlockSpec((1, d), lambda i, j: (0, 0)),
            pl.BlockSpec((1, d), lambda i, j: (0, 0)),
        ],
        out_specs=pl.BlockSpec((tm, d), lambda i, j: (i, 0)),
        out_shape=jax.ShapeDtypeStruct((t, d), F32),
        scratch_shapes=[pltpu.VMEM((tm, d), BF16)],
        compiler_params=_cparams(2),
        name="ffn_ln",
    )(x, wg, wu, wd, g, b)


def _rope_tables(max_len, head_dim):
    inv = ROPE_THETA ** (-jnp.arange(0, head_dim, 2, dtype=F32) / head_dim)
    ang = jnp.arange(max_len, dtype=F32)[:, None] * inv[None, :]
    ang = jnp.concatenate([ang, ang], -1)
    cos = jnp.cos(ang)
    sin = jnp.sin(ang)
    half = head_dim // 2
    sign = jnp.where(jnp.arange(head_dim) < half, -1.0, 1.0).astype(F32)
    sin = sin * sign[None, :]
    reps = LANES // head_dim
    return jnp.tile(cos, (1, reps)), jnp.tile(sin, (1, reps))


def _rope_lane_tile(y, cos, sin_signed, head_dim):
    half = head_dim // 2
    lane = lax.broadcasted_iota(jnp.int32, y.shape, 1)
    lo = (lane & (head_dim - 1)) < half
    fwd = pltpu.roll(y, half, 1)
    bwd = pltpu.roll(y, LANES - half, 1)
    return y * cos + jnp.where(lo, bwd, fwd) * sin_signed


def _proj_rope_kernel(x_ref, w_ref, cos_ref, sin_ref, o_ref, *, head_dim, n_scaled, scale):
    y = jnp.dot(x_ref[...].astype(BF16), w_ref[...], preferred_element_type=F32)
    cos = cos_ref[...]
    sin = sin_ref[...]
    j = pl.program_id(1)
    tiles_per_block = y.shape[1] // LANES
    for c in range(tiles_per_block):
        r = _rope_lane_tile(y[:, c * LANES:(c + 1) * LANES], cos, sin, head_dim)
        tile_idx = j * tiles_per_block + c
        r = r * jnp.where(tile_idx < n_scaled, scale, 1.0).astype(F32)
        o_ref[:, c * LANES:(c + 1) * LANES] = r.astype(o_ref.dtype)


def _proj_rope(x, w, cos, sin, *, head_dim, n_scaled, scale, tm, tn, seq_len):
    t, d = x.shape
    n = w.shape[1]
    pos_spec = pl.BlockSpec((tm, LANES), lambda i, j: (i % (seq_len // tm), 0))
    return pl.pallas_call(
        functools.partial(_proj_rope_kernel, head_dim=head_dim, n_scaled=n_scaled, scale=scale),
        grid=(t // tm, n // tn),
        in_specs=[
            pl.BlockSpec((tm, d), lambda i, j: (i, 0)),
            pl.BlockSpec((d, tn), lambda i, j: (0, j)),
            pos_spec,
            pos_spec,
        ],
        out_specs=pl.BlockSpec((tm, tn), lambda i, j: (i, j)),
        out_shape=jax.ShapeDtypeStruct((t, n), BF16),
        compiler_params=_cparams(2),
        name="proj_rope",
    )(x, w, cos, sin)


def _proj_t_kernel(x_ref, wt_ref, o_ref, *, tk):
    yt = lax.dot_general(wt_ref[...], x_ref[...].astype(BF16), (((1,), (1,)), ((), ())),
                         preferred_element_type=F32)
    for c in range(o_ref.shape[0]):
        o_ref[c] = yt[:, c * tk:(c + 1) * tk].astype(o_ref.dtype)


def _proj_t(x, wt, *, tm, tk):
    t, d = x.shape
    n = wt.shape[0]
    assert tm % tk == 0
    return pl.pallas_call(
        functools.partial(_proj_t_kernel, tk=tk),
        grid=(t // tm,),
        in_specs=[
            pl.BlockSpec((tm, d), lambda i: (i, 0)),
            pl.BlockSpec((n, d), lambda i: (0, 0)),
        ],
        out_specs=pl.BlockSpec((tm // tk, n, tk), lambda i: (i, 0, 0)),
        out_shape=jax.ShapeDtypeStruct((t // tk, n, tk), BF16),
        compiler_params=_cparams(1),
        name="proj_t",
    )(x, wt)


ONES_ROWS = 16
DA_CHUNKS_PER_BODY = 4
DA_BODY_UNROLL = 1


def _diff_attn_kernel(q_ref, k_ref, vt_ref, lq1_ref, lk1_ref, lq2_ref, lk2_ref, g_ref, o_ref,
                      qb_ref, s_ref, cmax_ref, p_ref, a_ref, m_ref, acc_ref,
                      *, tq, tk, nk, group, hd, dv, lam_init):
    q = q_ref[...]
    lane = lax.broadcasted_iota(jnp.int32, q.shape, 1)
    zero = jnp.zeros_like(q)
    qb_ref[0:tq, :] = jnp.where(lane < hd, q, zero)
    qb_ref[tq:2 * tq, :] = jnp.where(lane >= hd, q, zero)
    ones = jnp.ones((ONES_ROWS, tk), BF16)

    def scores(i, slot):
        off = pl.multiple_of(i * tk, tk)
        k = k_ref[pl.ds(off, tk), :]
        s = lax.dot_general(k, qb_ref[...], (((1,), (1,)), ((), ())),
                            preferred_element_type=F32)
        s_ref[slot] = s
        cmax_ref[slot] = jnp.max(s, axis=0, keepdims=True)

    def softmax(slot):
        for c in range(2):
            m_old = m_ref[c]
            m_new = jnp.maximum(m_old, cmax_ref[slot, :, c * tq:(c + 1) * tq])
            a_ref[slot, c] = jnp.exp2(m_old - m_new)
            p_ref[slot, c] = jnp.exp2(s_ref[slot, :, c * tq:(c + 1) * tq] - m_new).astype(BF16)
            m_ref[c] = m_new

    def values(i, slot):
        vt = jnp.concatenate([vt_ref[i], ones], axis=0)
        for c in range(2):
            acc_ref[c] = a_ref[slot, c] * acc_ref[c] + jnp.dot(vt, p_ref[slot, c],
                                                               preferred_element_type=F32)

    m_ref[...] = jnp.full(m_ref.shape, -jnp.inf, F32)
    acc_ref[...] = jnp.zeros(acc_ref.shape, F32)

    def body(j, with_values=True, with_scores=True):
        if with_values:
            for g in range(group):
                values((j - 1) * group + g, g)
        for g in range(group):
            softmax(g)
        if with_scores:
            for g in range(group):
                scores((j + 1) * group + g, g)

    n_body = nk // group
    for g in range(group):
        scores(g, g)
    body(0, with_values=False)

    def loop_body(j, carry):
        body(j)
        return carry

    lax.fori_loop(1, n_body - 1, loop_body, 0, unroll=DA_BODY_UNROLL)
    body(n_body - 1, with_scores=False)
    for g in range(group):
        values(nk - group + g, g)

    lam = (jnp.exp(jnp.sum(lq1_ref[...] * lk1_ref[...], axis=-1, keepdims=True))
           - jnp.exp(jnp.sum(lq2_ref[...] * lk2_ref[...], axis=-1, keepdims=True)) + lam_init)
    o = (acc_ref[0, 0:dv] * (1.0 / acc_ref[0, dv:dv + 1])
         - lam * (acc_ref[1, 0:dv] * (1.0 / acc_ref[1, dv:dv + 1])))
    ms = jnp.mean(o * o, axis=0, keepdims=True)
    o = o * lax.rsqrt(ms + LN_EPS) * g_ref[...] * (1.0 - lam_init)
    o_ref[...] = o.T.astype(o_ref.dtype)


def _diff_attn(qk, vt, lq1, lk1, lq2, lk2, g_col, *, seq_len, n_heads, tq, tk, lam_init):
    dv = g_col.shape[0]
    hd = lq1.shape[1]
    n_seq = qk.shape[0] // seq_len
    nq = seq_len // tq
    nk = seq_len // tk
    group = min(DA_CHUNKS_PER_BODY, nk // 2)
    assert nk % group == 0 and nk // group >= 2, "the chunk loop is software-pipelined in groups"
    kernel = functools.partial(_diff_attn_kernel, tq=tq, tk=tk, nk=nk, group=group, hd=hd, dv=dv,
                               lam_init=lam_init)
    vec = pl.BlockSpec((1, hd), lambda b, h, i: (0, 0))
    return pl.pallas_call(
        kernel,
        grid=(n_seq, n_heads, nq),
        in_specs=[
            pl.BlockSpec((tq, 2 * hd), lambda b, h, i: (b * nq + i, h)),
            pl.BlockSpec((seq_len, 2 * hd), lambda b, h, i: (b, n_heads + h)),
            pl.BlockSpec((nk, dv, tk), lambda b, h, i: (b, h, 0)),
            vec, vec, vec, vec,
            pl.BlockSpec((dv, 1), lambda b, h, i: (0, 0)),
        ],
        out_specs=pl.BlockSpec((tq, dv), lambda b, h, i: (b * nq + i, h)),
        out_shape=jax.ShapeDtypeStruct((qk.shape[0], n_heads * dv), BF16),
        scratch_shapes=[
            pltpu.VMEM((2 * tq, 2 * hd), BF16),
            pltpu.VMEM((group, tk, 2 * tq), F32),
            pltpu.VMEM((group, 1, 2 * tq), F32),
            pltpu.VMEM((group, 2, tk, tq), BF16),
            pltpu.VMEM((group, 2, 1, tq), F32),
            pltpu.VMEM((2, 1, tq), F32),
            pltpu.VMEM((2, dv + ONES_ROWS, tq), F32),
        ],
        compiler_params=_cparams(3),
        name="diff_attn",
    )(qk, qk, vt, lq1, lk1, lq2, lk2, g_col)


def _window_attn_kernel(sink_ref, q_ref, k_ref, kp_ref, kn_ref, vt_ref, vtp_ref, vtn_ref, o_ref,
                        kwin_ref, *, tq, n_kv, group, hd, seq_len):
    w = WG_WINDOW
    nb = tq // w
    pos0 = lax.rem(pl.program_id(0) * tq, jnp.int32(seq_len))

    kwin_ref[0:w, :] = kp_ref[...]
    kwin_ref[w:w + tq, :] = k_ref[...]
    kwin_ref[w + tq:w + tq + w, :] = kn_ref[...]

    def vt_block(j, g):
        ref, idx = (vtp_ref, 0) if j < 0 else (vtn_ref, 0) if j >= nb else (vt_ref, j)
        return ref[idx, g * hd:(g + 1) * hd, :]

    cols = group * w
    c_idx = lax.broadcasted_iota(jnp.int32, (3 * w, cols), 0)
    a_idx = lax.broadcasted_iota(jnp.int32, (3 * w, cols), 1) & (w - 1)
    rel = c_idx - a_idx
    band = (rel >= 0) & (rel <= 2 * w)
    head_in_group = lax.broadcasted_iota(jnp.int32, (1, cols), 1) // w
    ones = jnp.ones((ONES_ROWS, 3 * w), BF16)
    log2e = math.log2(math.e)

    key_row = lax.broadcasted_iota(jnp.int32, (3 * w, 1), 0)

    for n in range(nb):
        kpos = pos0 + (n - 1) * w + key_row
        valid = band & ((kpos >= 0) & (kpos < seq_len))
        for g in range(n_kv):
            kw = kwin_ref[n * w:n * w + 3 * w, g * hd:(g + 1) * hd]
            qg = jnp.concatenate(
                [q_ref[n * w:(n + 1) * w, (g * group + j) * hd:(g * group + j + 1) * hd]
                 for j in range(group)], axis=0)
            s = lax.dot_general(kw, qg, (((1,), (1,)), ((), ())), preferred_element_type=F32)
            s = jnp.where(valid, s, -jnp.inf)
            sk = jnp.zeros((1, cols), F32)
            for j in range(group):
                sk = jnp.where(head_in_group == j, sink_ref[g * group + j] * log2e, sk)
            m = jnp.maximum(jnp.max(s, axis=0, keepdims=True), sk)
            p = jnp.exp2(s - m).astype(BF16)
            vt = jnp.concatenate([vt_block(n - 1, g), vt_block(n, g), vt_block(n + 1, g)], axis=1)
            ot = jnp.dot(jnp.concatenate([vt, ones], axis=0), p, preferred_element_type=F32)
            den = ot[hd:hd + 1, :] + jnp.exp2(sk - m)
            o = (ot[0:hd, :] * (1.0 / den)).T
            for j in range(group):
                h = g * group + j
                o_ref[n * w:(n + 1) * w, h * hd:(h + 1) * hd] = o[j * w:(j + 1) * w, :].astype(o_ref.dtype)


def _window_attn(q, k, vt, sink, *, n_heads, n_kv, tq, seq_len):
    t, qd = q.shape
    hd = qd // n_heads
    kd = n_kv * hd
    w = WG_WINDOW
    r = tq // w
    n_blk = t // w
    kernel = functools.partial(_window_attn_kernel, tq=tq, n_kv=n_kv, group=n_heads // n_kv, hd=hd,
                               seq_len=seq_len)
    prev_blk = lambda i: jnp.maximum(i * r - 1, 0)
    next_blk = lambda i: jnp.minimum((i + 1) * r, n_blk - 1)
    return pl.pallas_call(
        kernel,
        grid=(t // tq,),
        in_specs=[
            pl.BlockSpec(memory_space=pltpu.SMEM),
            pl.BlockSpec((tq, qd), lambda i: (i, 0)),
            pl.BlockSpec((tq, kd), lambda i: (i, 0)),
            pl.BlockSpec((w, kd), lambda i: (prev_blk(i), 0)),
            pl.BlockSpec((w, kd), lambda i: (next_blk(i), 0)),
            pl.BlockSpec((r, kd, w), lambda i: (i, 0, 0)),
            pl.BlockSpec((1, kd, w), lambda i: (prev_blk(i), 0, 0)),
            pl.BlockSpec((1, kd, w), lambda i: (next_blk(i), 0, 0)),
        ],
        out_specs=pl.BlockSpec((tq, qd), lambda i: (i, 0)),
        scratch_shapes=[pltpu.VMEM((tq + 2 * w, kd), BF16)],
        out_shape=jax.ShapeDtypeStruct((t, qd), BF16),
        compiler_params=_cparams(1),
        name="window_attn",
    )(sink, q, k, k, k, vt, vt, vt)


def _outproj_kernel(x_ref, o_ref, w_ref, g_ref, b_ref, y_ref, *, alpha):
    h = jnp.dot(o_ref[...], w_ref[...], preferred_element_type=F32)
    y_ref[...] = _layernorm_rows(alpha * x_ref[...] + h, g_ref[...], b_ref[...])


def _outproj(x, o, w, g, b, *, alpha, tm):
    t, d = x.shape
    k = o.shape[1]
    return pl.pallas_call(
        functools.partial(_outproj_kernel, alpha=alpha),
        grid=(t // tm,),
        in_specs=[
            pl.BlockSpec((tm, d), lambda i: (i, 0)),
            pl.BlockSpec((tm, k), lambda i: (i, 0)),
            pl.BlockSpec((k, d), lambda i: (0, 0)),
            pl.BlockSpec((1, d), lambda i: (0, 0)),
            pl.BlockSpec((1, d), lambda i: (0, 0)),
        ],
        out_specs=pl.BlockSpec((tm, d), lambda i: (i, 0)),
        out_shape=jax.ShapeDtypeStruct((t, d), F32),
        compiler_params=_cparams(1),
        name="outproj_ln",
    )(x, o, w, g, b)


def _largest_divisor(n, cap, multiple):
    best = None
    for c in range(multiple, min(n, cap) + 1, multiple):
        if n % c == 0:
            best = c
    assert best is not None, (n, cap, multiple)
    return best


def _tiles(seq_len, d_ff):
    return dict(
        tm=_largest_divisor(seq_len, 512, LANES),
        tmf=_largest_divisor(seq_len, 1024, LANES),
        tf=_largest_divisor(d_ff, 512, 2 * LANES),
        tq=_largest_divisor(seq_len, 1024, LANES),
        tk=_largest_divisor(seq_len // 2, 512, LANES),
        tw=_largest_divisor(seq_len, 512, WG_WINDOW),
    )


def kernel(x_prompt, x_sample, ffn1_w_gate, ffn1_w_up, ffn1_w_down, ffn2_w_gate, ffn2_w_up, ffn2_w_down, ln_g, ln_b, da_w_qkv, da_w_o, da_lambda_q1, da_lambda_k1, da_lambda_q2, da_lambda_k2, da_subln_g, wg_w_qkv, wg_w_o, wg_sink):
    bp, sp, d = x_prompt.shape
    bs, ss, _ = x_sample.shape
    depth = ffn1_w_gate.shape[0]
    d_ff = ffn1_w_gate.shape[2]
    alpha = (2.0 * depth) ** 0.25

    da_hd = da_lambda_q1.shape[1]
    da_dv = da_subln_g.shape[1]
    da_heads = d // da_dv
    da_qd = 2 * da_heads * da_hd
    wg_heads = wg_sink.shape[1]
    wg_hd = d // wg_heads
    wg_kv = (wg_w_qkv.shape[2] - d) // (2 * wg_hd)
    assert 2 * da_hd == LANES and da_dv == LANES and wg_hd == LANES

    cos_a, sin_a = _rope_tables(max(sp, ss), da_hd)
    cos_b, sin_b = _rope_tables(max(sp, ss), wg_hd)

    ffn_w = [[(wg[i].astype(BF16), wu[i].astype(BF16), wd[i].astype(BF16))
              for wg, wu, wd in ((ffn1_w_gate, ffn1_w_up, ffn1_w_down), (ffn2_w_gate, ffn2_w_up, ffn2_w_down))]
             for i in range(depth)]
    mix_w = []
    for i in range(depth):
        j = i // 2
        if i % 2 == 0:
            w = da_w_qkv[j]
            mix_w.append((w[:, :2 * da_qd].astype(BF16), w[:, 2 * da_qd:].T.astype(BF16), da_w_o[j].astype(BF16)))
        else:
            w = wg_w_qkv[j].astype(BF16)
            kd = wg_kv * wg_hd
            mix_w.append((w[:, :d], w[:, d:d + kd], w[:, d + kd:].T, wg_w_o[j].astype(BF16)))

    def ln(i, k):
        return ln_g[i, k][None, :], ln_b[i, k][None, :]

    def trunk(x, seq_len):
        tl = _tiles(seq_len, d_ff)
        tm, tmf, tf, tq, tk, tw = tl["tm"], tl["tmf"], tl["tf"], tl["tq"], tl["tk"], tl["tw"]
        for i in range(depth):
            x = _ffn(x, *ffn_w[i][0], *ln(i, 0), alpha=alpha, tm=tmf, tf=tf)
            j = i // 2
            if i % 2 == 0:
                lam_init = 0.8 - 0.6 * math.exp(-0.3 * i)
                w_qk, w_vt, w_o = mix_w[i]
                qk = _proj_rope(x, w_qk, cos_a, sin_a, head_dim=da_hd, n_scaled=da_qd // LANES,
                                scale=da_hd ** -0.5 * math.log2(math.e), tm=tm, tn=da_qd // 2,
                                seq_len=seq_len)
                vt = _proj_t(x, w_vt, tm=tm, tk=tk)
                o = _diff_attn(qk, vt, da_lambda_q1[j][None, :], da_lambda_k1[j][None, :],
                               da_lambda_q2[j][None, :], da_lambda_k2[j][None, :], da_subln_g[j][:, None],
                               seq_len=seq_len, n_heads=da_heads, tq=tq, tk=tk, lam_init=lam_init)
            else:
                w_q, w_k, w_vt, w_o = mix_w[i]
                q = _proj_rope(x, w_q, cos_b, sin_b, head_dim=wg_hd, n_scaled=d // LANES,
                               scale=wg_hd ** -0.5 * math.log2(math.e), tm=tm, tn=d // 2, seq_len=seq_len)
                k = _proj_rope(x, w_k, cos_b, sin_b, head_dim=wg_hd, n_scaled=0, scale=1.0, tm=tm,
                               tn=wg_kv * wg_hd, seq_len=seq_len)
                vt = _proj_t(x, w_vt, tm=tm, tk=WG_WINDOW)
                o = _window_attn(q, k, vt, wg_sink[j], n_heads=wg_heads, n_kv=wg_kv, tq=tw, seq_len=seq_len)
            x = _outproj(x, o, w_o, *ln(i, 1), alpha=alpha, tm=tm)
            x = _ffn(x, *ffn_w[i][1], *ln(i, 2), alpha=alpha, tm=tmf, tf=tf)
        return x

    y_prompt = trunk(x_prompt.reshape(bp * sp, d), sp).reshape(bp, sp, d)
    y_sample = trunk(x_sample.reshape(bs * ss, d), ss).reshape(bs, ss, d)
    return (y_prompt, y_sample)
```

```python
import functools
import math

import jax
import jax.numpy as jnp
from jax import lax
from jax.experimental import pallas as pl
from jax.experimental.pallas import tpu as pltpu

F32 = jnp.float32
BF16 = jnp.bfloat16

LN_EPS = 1e-5
ROPE_THETA = 10000.0
WG_WINDOW = 128
LANES = 128
V7X_VMEM_LIMIT_BYTES = 60000 * 1024


def _cparams(n_axes):
    return pltpu.CompilerParams(
        dimension_semantics=("arbitrary",) * n_axes,
        vmem_limit_bytes=V7X_VMEM_LIMIT_BYTES,
    )


def _layernorm_rows(y, g, b):
    mu = jnp.mean(y, axis=-1, keepdims=True)
    yc = y - mu
    var = jnp.mean(yc * yc, axis=-1, keepdims=True)
    return yc * lax.rsqrt(var + LN_EPS) * g + b


FFN_ROW_SUBTILE = 512
LN_ROW_SUBTILE = 128


def _ffn_kernel(x_ref, wg_ref, wu_ref, wd_ref, g_ref, b_ref, o_ref, xb_ref, *, alpha, nf):
    f = pl.program_id(1)
    tm = xb_ref.shape[0]
    tr = min(FFN_ROW_SUBTILE, tm)
    tl = min(LN_ROW_SUBTILE, tr)

    def step(first, last):
        for r in range(0, tm, tr):
            if first:
                xb_ref[r:r + tr, :] = x_ref[r:r + tr, :].astype(BF16)
            xb = xb_ref[r:r + tr, :]
            hg = jnp.dot(xb, wg_ref[...], preferred_element_type=F32)
            hu = jnp.dot(xb, wu_ref[...], preferred_element_type=F32)
            a = (hg * jax.nn.sigmoid(hg) * hu).astype(BF16)
            part = jnp.dot(a, wd_ref[...], preferred_element_type=F32)
            if first:
                o_ref[r:r + tr, :] = part
            else:
                o_ref[r:r + tr, :] += part
            if last:
                for q in range(r, r + tr, tl):
                    y = alpha * x_ref[q:q + tl, :] + 0.5 * o_ref[q:q + tl, :]
                    o_ref[q:q + tl, :] = _layernorm_rows(y, g_ref[...], b_ref[...])

    if nf == 1:
        step(True, True)
    else:
        pl.when(f == 0)(lambda: step(True, False))
        pl.when((f > 0) & (f < nf - 1))(lambda: step(False, False))
        pl.when(f == nf - 1)(lambda: step(False, True))


def _ffn(x, wg, wu, wd, g, b, *, alpha, tm, tf):
    t, d = x.shape
    f = wg.shape[1]
    nf = f // tf
    return pl.pallas_call(
        functools.partial(_ffn_kernel, alpha=alpha, nf=nf),
        grid=(t // tm, nf),
        in_specs=[
            pl.BlockSpec((tm, d), lambda i, j: (i, 0)),
            pl.BlockSpec((d, tf), lambda i, j: (0, j)),
            pl.BlockSpec((d, tf), lambda i, j: (0, j)),
            pl.BlockSpec((tf, d), lambda i, j: (j, 0)),
            pl.BlockSpec((1, d), lambda i, j: (0, 0)),
            pl.BlockSpec((1, d), lambda i, j: (0, 0)),
        ],
        out_specs=pl.BlockSpec((tm, d), lambda i, j: (i, 0)),
        out_shape=jax.ShapeDtypeStruct((t, d), F32),
        scratch_shapes=[pltpu.VMEM((tm, d), BF16)],
        compiler_params=_cparams(2),
        name="ffn_ln",
    )(x, wg, wu, wd, g, b)


def _rope_tables(max_len, head_dim):
    inv = ROPE_THETA ** (-jnp.arange(0, head_dim, 2, dtype=F32) / head_dim)
    ang = jnp.arange(max_len, dtype=F32)[:, None] * inv[None, :]
    ang = jnp.concatenate([ang, ang], -1)
    cos = jnp.cos(ang)
    sin = jnp.sin(ang)
    half = head_dim // 2
    sign = jnp.where(jnp.arange(head_dim) < half, -1.0, 1.0).astype(F32)
    sin = sin * sign[None, :]
    reps = LANES // head_dim
    return jnp.tile(cos, (1, reps)), jnp.tile(sin, (1, reps))


def _rope_lane_tile(y, cos, sin_signed, head_dim):
    half = head_dim // 2
    lane = lax.broadcasted_iota(jnp.int32, y.shape, 1)
    lo = (lane & (head_dim - 1)) < half
    fwd = pltpu.roll(y, half, 1)
    bwd = pltpu.roll(y, LANES - half, 1)
    return y * cos + jnp.where(lo, bwd, fwd) * sin_signed


PROJ_ROW_SUBTILE = 512


def _proj_rope_kernel(x_ref, w_ref, cos_ref, sin_ref, o_ref, *, head_dim, n_scaled, scale):
    j = pl.program_id(1)
    tm = x_ref.shape[0]
    tr = min(PROJ_ROW_SUBTILE, tm)
    tiles_per_block = o_ref.shape[1] // LANES
    for r0 in range(0, tm, tr):
        y = jnp.dot(x_ref[r0:r0 + tr, :].astype(BF16), w_ref[...], preferred_element_type=F32)
        cos = cos_ref[r0:r0 + tr, :]
        sin = sin_ref[r0:r0 + tr, :]
        for c in range(tiles_per_block):
            r = _rope_lane_tile(y[:, c * LANES:(c + 1) * LANES], cos, sin, head_dim)
            tile_idx = j * tiles_per_block + c
            r = r * jnp.where(tile_idx < n_scaled, scale, 1.0).astype(F32)
            o_ref[r0:r0 + tr, c * LANES:(c + 1) * LANES] = r.astype(o_ref.dtype)


def _proj_rope(x, w, cos, sin, *, head_dim, n_scaled, scale, tm, tn, seq_len):
    t, d = x.shape
    n = w.shape[1]
    pos_spec = pl.BlockSpec((tm, LANES), lambda i, j: (i % (seq_len // tm), 0))
    return pl.pallas_call(
        functools.partial(_proj_rope_kernel, head_dim=head_dim, n_scaled=n_scaled, scale=scale),
        grid=(t // tm, n // tn),
        in_specs=[
            pl.BlockSpec((tm, d), lambda i, j: (i, 0)),
            pl.BlockSpec((d, tn), lambda i, j: (0, j)),
            pos_spec,
            pos_spec,
        ],
        out_specs=pl.BlockSpec((tm, tn), lambda i, j: (i, j)),
        out_shape=jax.ShapeDtypeStruct((t, n), BF16),
        compiler_params=_cparams(2),
        name="proj_rope",
    )(x, w, cos, sin)


def _proj_t_kernel(x_ref, wt_ref, o_ref, *, tk):
    yt = lax.dot_general(wt_ref[...], x_ref[...].astype(BF16), (((1,), (1,)), ((), ())),
                         preferred_element_type=F32)
    for c in range(o_ref.shape[0]):
        o_ref[c] = yt[:, c * tk:(c + 1) * tk].astype(o_ref.dtype)


def _proj_tr_kernel(x_ref, w_ref, o_ref, y_ref, *, tk):
    y_ref[...] = jnp.dot(x_ref[...].astype(BF16), w_ref[...], preferred_element_type=F32)
    for c in range(o_ref.shape[0]):
        o_ref[c] = y_ref[c * tk:(c + 1) * tk, :].T.astype(o_ref.dtype)


def _proj_t(x, w, *, tm, tk, weight_is_transposed):
    t, d = x.shape
    n = w.shape[0] if weight_is_transposed else w.shape[1]
    assert tm % tk == 0
    kernel = _proj_t_kernel if weight_is_transposed else _proj_tr_kernel
    return pl.pallas_call(
        functools.partial(kernel, tk=tk),
        grid=(t // tm,),
        in_specs=[
            pl.BlockSpec((tm, d), lambda i: (i, 0)),
            pl.BlockSpec(w.shape, lambda i: (0, 0)),
        ],
        out_specs=pl.BlockSpec((tm // tk, n, tk), lambda i: (i, 0, 0)),
        out_shape=jax.ShapeDtypeStruct((t // tk, n, tk), BF16),
        scratch_shapes=[] if weight_is_transposed else [pltpu.VMEM((tm, n), F32)],
        compiler_params=_cparams(1),
        name="proj_t",
    )(x, w)


ONES_ROWS = 16
DA_CHUNKS_PER_BODY = 4
DA_BODY_UNROLL = 1


def _diff_attn_kernel(q_ref, k_ref, vt_ref, lq1_ref, lk1_ref, lq2_ref, lk2_ref, g_ref, o_ref,
                      qb_ref, s_ref, cmax_ref, p_ref, a_ref, m_ref, acc_ref,
                      *, tq, tk, nk, group, hd, dv, lam_init):
    q = q_ref[...]
    lane = lax.broadcasted_iota(jnp.int32, q.shape, 1)
    zero = jnp.zeros_like(q)
    qb_ref[0:tq, :] = jnp.where(lane < hd, q, zero)
    qb_ref[tq:2 * tq, :] = jnp.where(lane >= hd, q, zero)
    ones = jnp.ones((ONES_ROWS, tk), BF16)

    def scores(i, slot):
        off = pl.multiple_of(i * tk, tk)
        k = k_ref[pl.ds(off, tk), :]
        s = lax.dot_general(k, qb_ref[...], (((1,), (1,)), ((), ())),
                            preferred_element_type=F32)
        s_ref[slot] = s
        cmax_ref[slot] = jnp.max(s, axis=0, keepdims=True)

    def softmax(slot):
        for c in range(2):
            m_old = m_ref[c]
            m_new = jnp.maximum(m_old, cmax_ref[slot, :, c * tq:(c + 1) * tq])
            a_ref[slot, c] = jnp.exp2(m_old - m_new)
            p_ref[slot, c] = jnp.exp2(s_ref[slot, :, c * tq:(c + 1) * tq] - m_new).astype(BF16)
            m_ref[c] = m_new

    def values(i, slot):
        vt = jnp.concatenate([vt_ref[i], ones], axis=0)
        for c in range(2):
            acc_ref[c] = a_ref[slot, c] * acc_ref[c] + jnp.dot(vt, p_ref[slot, c],
                                                               preferred_element_type=F32)

    m_ref[...] = jnp.full(m_ref.shape, -jnp.inf, F32)
    acc_ref[...] = jnp.zeros(acc_ref.shape, F32)

    def body(j, with_values=True, with_scores=True):
        if with_values:
            for g in range(group):
                values((j - 1) * group + g, g)
        for g in range(group):
            softmax(g)
        if with_scores:
            for g in range(group):
                scores((j + 1) * group + g, g)

    n_body = nk // group
    for g in range(group):
        scores(g, g)
    body(0, with_values=False)

    def loop_body(j, carry):
        body(j)
        return carry

    lax.fori_loop(1, n_body - 1, loop_body, 0, unroll=DA_BODY_UNROLL)
    body(n_body - 1, with_scores=False)
    for g in range(group):
        values(nk - group + g, g)

    lam = (jnp.exp(jnp.sum(lq1_ref[...] * lk1_ref[...], axis=-1, keepdims=True))
           - jnp.exp(jnp.sum(lq2_ref[...] * lk2_ref[...], axis=-1, keepdims=True)) + lam_init)
    o = (acc_ref[0, 0:dv] * (1.0 / acc_ref[0, dv:dv + 1])
         - lam * (acc_ref[1, 0:dv] * (1.0 / acc_ref[1, dv:dv + 1])))
    ms = jnp.mean(o * o, axis=0, keepdims=True)
    o = o * lax.rsqrt(ms + LN_EPS) * g_ref[...] * (1.0 - lam_init)
    o_ref[...] = o.T.astype(o_ref.dtype)


def _diff_attn(qk, vt, lq1, lk1, lq2, lk2, g_col, *, seq_len, n_heads, tq, tk, lam_init):
    dv = g_col.shape[0]
    hd = lq1.shape[1]
    n_seq = qk.shape[0] // seq_len
    nq = seq_len // tq
    nk = seq_len // tk
    group = min(DA_CHUNKS_PER_BODY, nk // 2)
    assert nk % group == 0 and nk // group >= 2, "the chunk loop is software-pipelined in groups"
    kernel = functools.partial(_diff_attn_kernel, tq=tq, tk=tk, nk=nk, group=group, hd=hd, dv=dv,
                               lam_init=lam_init)
    vec = pl.BlockSpec((1, hd), lambda b, h, i: (0, 0))
    return pl.pallas_call(
        kernel,
        grid=(n_seq, n_heads, nq),
        in_specs=[
            pl.BlockSpec((tq, 2 * hd), lambda b, h, i: (b * nq + i, h)),
            pl.BlockSpec((seq_len, 2 * hd), lambda b, h, i: (b, n_heads + h)),
            pl.BlockSpec((nk, dv, tk), lambda b, h, i: (b, h, 0)),
            vec, vec, vec, vec,
            pl.BlockSpec((dv, 1), lambda b, h, i: (0, 0)),
        ],
        out_specs=pl.BlockSpec((tq, dv), lambda b, h, i: (b * nq + i, h)),
        out_shape=jax.ShapeDtypeStruct((qk.shape[0], n_heads * dv), BF16),
        scratch_shapes=[
            pltpu.VMEM((2 * tq, 2 * hd), BF16),
            pltpu.VMEM((group, tk, 2 * tq), F32),
            pltpu.VMEM((group, 1, 2 * tq), F32),
            pltpu.VMEM((group, 2, tk, tq), BF16),
            pltpu.VMEM((group, 2, 1, tq), F32),
            pltpu.VMEM((2, 1, tq), F32),
            pltpu.VMEM((2, dv + ONES_ROWS, tq), F32),
        ],
        compiler_params=_cparams(3),
        name="diff_attn",
    )(qk, qk, vt, lq1, lk1, lq2, lk2, g_col)


def _window_attn_kernel(sink_ref, q_ref, k_ref, kp_ref, kn_ref, vt_ref, vtp_ref, vtn_ref, o_ref,
                        kwin_ref, *, tq, n_kv, group, hd, seq_len):
    w = WG_WINDOW
    nb = tq // w
    pos0 = lax.rem(pl.program_id(0) * tq, jnp.int32(seq_len))

    kwin_ref[0:w, :] = kp_ref[...]
    kwin_ref[w:w + tq, :] = k_ref[...]
    kwin_ref[w + tq:w + tq + w, :] = kn_ref[...]

    def vt_block(j, g):
        ref, idx = (vtp_ref, 0) if j < 0 else (vtn_ref, 0) if j >= nb else (vt_ref, j)
        return ref[idx, g * hd:(g + 1) * hd, :]

    cols = group * w
    c_idx = lax.broadcasted_iota(jnp.int32, (3 * w, cols), 0)
    a_idx = lax.broadcasted_iota(jnp.int32, (3 * w, cols), 1) & (w - 1)
    rel = c_idx - a_idx
    band = (rel >= 0) & (rel <= 2 * w)
    head_in_group = lax.broadcasted_iota(jnp.int32, (1, cols), 1) // w
    ones = jnp.ones((ONES_ROWS, 3 * w), BF16)
    log2e = math.log2(math.e)

    key_row = lax.broadcasted_iota(jnp.int32, (3 * w, 1), 0)

    for n in range(nb):
        kpos = pos0 + (n - 1) * w + key_row
        valid = band & ((kpos >= 0) & (kpos < seq_len))
        for g in range(n_kv):
            kw = kwin_ref[n * w:n * w + 3 * w, g * hd:(g + 1) * hd]
            qg = jnp.concatenate(
                [q_ref[n * w:(n + 1) * w, (g * group + j) * hd:(g * group + j + 1) * hd]
                 for j in range(group)], axis=0)
            s = lax.dot_general(kw, qg, (((1,), (1,)), ((), ())), preferred_element_type=F32)
            s = jnp.where(valid, s, -jnp.inf)
            sk = jnp.zeros((1, cols), F32)
            for j in range(group):
                sk = jnp.where(head_in_group == j, sink_ref[g * group + j] * log2e, sk)
            m = jnp.maximum(jnp.max(s, axis=0, keepdims=True), sk)
            p = jnp.exp2(s - m).astype(BF16)
            vt = jnp.concatenate([vt_block(n - 1, g), vt_block(n, g), vt_block(n + 1, g)], axis=1)
            ot = jnp.dot(jnp.concatenate([vt, ones], axis=0), p, preferred_element_type=F32)
            den = ot[hd:hd + 1, :] + jnp.exp2(sk - m)
            o = (ot[0:hd, :] * (1.0 / den)).T
            for j in range(group):
                h = g * group + j
                o_ref[n * w:(n + 1) * w, h * hd:(h + 1) * hd] = o[j * w:(j + 1) * w, :].astype(o_ref.dtype)


def _window_attn(q, k, vt, sink, *, n_heads, n_kv, tq, seq_len):
    t, qd = q.shape
    hd = qd // n_heads
    kd = n_kv * hd
    w = WG_WINDOW
    r = tq // w
    n_blk = t // w
    kernel = functools.partial(_window_attn_kernel, tq=tq, n_kv=n_kv, group=n_heads // n_kv, hd=hd,
                               seq_len=seq_len)
    prev_blk = lambda i: jnp.maximum(i * r - 1, 0)
    next_blk = lambda i: jnp.minimum((i + 1) * r, n_blk - 1)
    return pl.pallas_call(
        kernel,
        grid=(t // tq,),
        in_specs=[
            pl.BlockSpec(memory_space=pltpu.SMEM),
            pl.BlockSpec((tq, qd), lambda i: (i, 0)),
            pl.BlockSpec((tq, kd), lambda i: (i, 0)),
            pl.BlockSpec((w, kd), lambda i: (prev_blk(i), 0)),
            pl.BlockSpec((w, kd), lambda i: (next_blk(i), 0)),
            pl.BlockSpec((r, kd, w), lambda i: (i, 0, 0)),
            pl.BlockSpec((1, kd, w), lambda i: (prev_blk(i), 0, 0)),
            pl.BlockSpec((1, kd, w), lambda i: (next_blk(i), 0, 0)),
        ],
        out_specs=pl.BlockSpec((tq, qd), lambda i: (i, 0)),
        scratch_shapes=[pltpu.VMEM((tq + 2 * w, kd), BF16)],
        out_shape=jax.ShapeDtypeStruct((t, qd), BF16),
        compiler_params=_cparams(1),
        name="window_attn",
    )(sink, q, k, k, k, vt, vt, vt)


def _outproj_kernel(x_ref, o_ref, w_ref, g_ref, b_ref, y_ref, *, alpha):
    h = jnp.dot(o_ref[...], w_ref[...], preferred_element_type=F32)
    y_ref[...] = _layernorm_rows(alpha * x_ref[...] + h, g_ref[...], b_ref[...])


def _outproj(x, o, w, g, b, *, alpha, tm):
    t, d = x.shape
    k = o.shape[1]
    return pl.pallas_call(
        functools.partial(_outproj_kernel, alpha=alpha),
        grid=(t // tm,),
        in_specs=[
            pl.BlockSpec((tm, d), lambda i: (i, 0)),
            pl.BlockSpec((tm, k), lambda i: (i, 0)),
            pl.BlockSpec((k, d), lambda i: (0, 0)),
            pl.BlockSpec((1, d), lambda i: (0, 0)),
            pl.BlockSpec((1, d), lambda i: (0, 0)),
        ],
        out_specs=pl.BlockSpec((tm, d), lambda i: (i, 0)),
        out_shape=jax.ShapeDtypeStruct((t, d), F32),
        compiler_params=_cparams(1),
        name="outproj_ln",
    )(x, o, w, g, b)


def _largest_divisor(n, cap, multiple):
    best = None
    for c in range(multiple, min(n, cap) + 1, multiple):
        if n % c == 0:
            best = c
    assert best is not None, (n, cap, multiple)
    return best


def _tiles(seq_len, d_ff):
    return dict(
        tm=_largest_divisor(seq_len, 512, LANES),
        tmp=_largest_divisor(seq_len, 1024, LANES),
        tmf=_largest_divisor(seq_len, 1024, LANES),
        tf=_largest_divisor(d_ff, 512, 2 * LANES),
        tq=_largest_divisor(seq_len, 1024, LANES),
        tk=_largest_divisor(seq_len // 2, 512, LANES),
        tw=_largest_divisor(seq_len, 512, WG_WINDOW),
    )


def kernel(x_prompt, x_sample, ffn1_w_gate, ffn1_w_up, ffn1_w_down, ffn2_w_gate, ffn2_w_up, ffn2_w_down, ln_g, ln_b, da_w_qkv, da_w_o, da_lambda_q1, da_lambda_k1, da_lambda_q2, da_lambda_k2, da_subln_g, wg_w_qkv, wg_w_o, wg_sink):
    bp, sp, d = x_prompt.shape
    bs, ss, _ = x_sample.shape
    depth = ffn1_w_gate.shape[0]
    d_ff = ffn1_w_gate.shape[2]
    alpha = (2.0 * depth) ** 0.25

    da_hd = da_lambda_q1.shape[1]
    da_dv = da_subln_g.shape[1]
    da_heads = d // da_dv
    da_qd = 2 * da_heads * da_hd
    wg_heads = wg_sink.shape[1]
    wg_hd = d // wg_heads
    wg_kv = (wg_w_qkv.shape[2] - d) // (2 * wg_hd)
    assert 2 * da_hd == LANES and da_dv == LANES and wg_hd == LANES

    cos_a, sin_a = _rope_tables(max(sp, ss), da_hd)
    cos_b, sin_b = _rope_tables(max(sp, ss), wg_hd)

    ffn_w = [[(wg[i].astype(BF16), wu[i].astype(BF16), wd[i].astype(BF16))
              for wg, wu, wd in ((ffn1_w_gate, ffn1_w_up, ffn1_w_down), (ffn2_w_gate, ffn2_w_up, ffn2_w_down))]
             for i in range(depth)]
    mix_w = []
    for i in range(depth):
        j = i // 2
        if i % 2 == 0:
            w = da_w_qkv[j]
            mix_w.append((w[:, :2 * da_qd].astype(BF16), w[:, 2 * da_qd:].T.astype(BF16), da_w_o[j].astype(BF16)))
        else:
            w = wg_w_qkv[j].astype(BF16)
            kd = wg_kv * wg_hd
            mix_w.append((w[:, :d], w[:, d:d + kd], w[:, d + kd:], wg_w_o[j].astype(BF16)))

    def ln(i, k):
        return ln_g[i, k][None, :], ln_b[i, k][None, :]

    def trunk(x, seq_len):
        tl = _tiles(seq_len, d_ff)
        tm, tmp, tmf, tf, tq, tk, tw = (tl[n] for n in ("tm", "tmp", "tmf", "tf", "tq", "tk", "tw"))
        for i in range(depth):
            x = _ffn(x, *ffn_w[i][0], *ln(i, 0), alpha=alpha, tm=tmf, tf=tf)
            j = i // 2
            if i % 2 == 0:
                lam_init = 0.8 - 0.6 * math.exp(-0.3 * i)
                w_qk, w_vt, w_o = mix_w[i]
                qk = _proj_rope(x, w_qk, cos_a, sin_a, head_dim=da_hd, n_scaled=da_qd // LANES,
                                scale=da_hd ** -0.5 * math.log2(math.e), tm=tmp, tn=da_qd // 2,
                                seq_len=seq_len)
                vt = _proj_t(x, w_vt, tm=tm, tk=tk, weight_is_transposed=True)
                o = _diff_attn(qk, vt, da_lambda_q1[j][None, :], da_lambda_k1[j][None, :],
                               da_lambda_q2[j][None, :], da_lambda_k2[j][None, :], da_subln_g[j][:, None],
                               seq_len=seq_len, n_heads=da_heads, tq=tq, tk=tk, lam_init=lam_init)
            else:
                w_q, w_k, w_v, w_o = mix_w[i]
                q = _proj_rope(x, w_q, cos_b, sin_b, head_dim=wg_hd, n_scaled=d // LANES,
                               scale=wg_hd ** -0.5 * math.log2(math.e), tm=tmp, tn=d // 2, seq_len=seq_len)
                k = _proj_rope(x, w_k, cos_b, sin_b, head_dim=wg_hd, n_scaled=0, scale=1.0, tm=tmp,
                               tn=wg_kv * wg_hd, seq_len=seq_len)
                vt = _proj_t(x, w_v, tm=tmp, tk=WG_WINDOW, weight_is_transposed=False)
                o = _window_attn(q, k, vt, wg_sink[j], n_heads=wg_heads, n_kv=wg_kv, tq=tw, seq_len=seq_len)
            x = _outproj(x, o, w_o, *ln(i, 1), alpha=alpha, tm=tm)
            x = _ffn(x, *ffn_w[i][1], *ln(i, 2), alpha=alpha, tm=tmf, tf=tf)
        return x

    y_prompt = trunk(x_prompt.reshape(bp * sp, d), sp).reshape(bp, sp, d)
    y_sample = trunk(x_sample.reshape(bs * ss, d), ss).reshape(bs, ss, d)
    return (y_prompt, y_sample)
```

```python
import functools
import math

import jax
import jax.numpy as jnp
from jax import lax
from jax.experimental import pallas as pl
from jax.experimental.pallas import tpu as pltpu

F32 = jnp.float32
BF16 = jnp.bfloat16

LN_EPS = 1e-5
ROPE_THETA = 10000.0
WG_WINDOW = 128
LANES = 128
V7X_VMEM_LIMIT_BYTES = 60000 * 1024


def _cparams(n_axes):
    return pltpu.CompilerParams(
        dimension_semantics=("arbitrary",) * n_axes,
        vmem_limit_bytes=V7X_VMEM_LIMIT_BYTES,
    )


def _layernorm_rows(y, g, b):
    mu = jnp.mean(y, axis=-1, keepdims=True)
    yc = y - mu
    var = jnp.mean(yc * yc, axis=-1, keepdims=True)
    return yc * lax.rsqrt(var + LN_EPS) * g + b


FFN_ROW_SUBTILE = 512
LN_ROW_SUBTILE = 128


def _ffn_kernel(x_ref, wg_ref, wu_ref, wd_ref, g_ref, b_ref, o_ref, xb_ref, *, alpha, nf):
    f = pl.program_id(1)
    tm = xb_ref.shape[0]
    tr = min(FFN_ROW_SUBTILE, tm)
    tl = min(LN_ROW_SUBTILE, tr)

    def step(first, last):
        for r in range(0, tm, tr):
            if first:
                xb_ref[r:r + tr, :] = x_ref[r:r + tr, :].astype(BF16)
            xb = xb_ref[r:r + tr, :]
            hg = jnp.dot(xb, wg_ref[...], preferred_element_type=F32)
            hu = jnp.dot(xb, wu_ref[...], preferred_element_type=F32)
            a = (hg * jax.nn.sigmoid(hg) * hu).astype(BF16)
            part = jnp.dot(a, wd_ref[...], preferred_element_type=F32)
            if first:
                o_ref[r:r + tr, :] = part
            else:
                o_ref[r:r + tr, :] += part
            if last:
                for q in range(r, r + tr, tl):
                    y = alpha * x_ref[q:q + tl, :] + 0.5 * o_ref[q:q + tl, :]
                    o_ref[q:q + tl, :] = _layernorm_rows(y, g_ref[...], b_ref[...])

    if nf == 1:
        step(True, True)
    else:
        pl.when(f == 0)(lambda: step(True, False))
        pl.when((f > 0) & (f < nf - 1))(lambda: step(False, False))
        pl.when(f == nf - 1)(lambda: step(False, True))


def _ffn(x, wg, wu, wd, g, b, *, layer, alpha, tm, tf):
    t, d = x.shape
    f = wg.shape[2]
    nf = f // tf
    return pl.pallas_call(
        functools.partial(_ffn_kernel, alpha=alpha, nf=nf),
        grid=(t // tm, nf),
        in_specs=[
            pl.BlockSpec((tm, d), lambda i, j: (i, 0)),
            pl.BlockSpec((None, d, tf), lambda i, j: (layer, 0, j)),
            pl.BlockSpec((None, d, tf), lambda i, j: (layer, 0, j)),
            pl.BlockSpec((None, tf, d), lambda i, j: (layer, j, 0)),
            pl.BlockSpec((1, d), lambda i, j: (0, 0)),
            pl.BlockSpec((1, d), lambda i, j: (0, 0)),
        ],
        out_specs=pl.BlockSpec((tm, d), lambda i, j: (i, 0)),
        out_shape=jax.ShapeDtypeStruct((t, d), F32),
        scratch_shapes=[pltpu.VMEM((tm, d), BF16)],
        compiler_params=_cparams(2),
        name="ffn_ln",
    )(x, wg, wu, wd, g, b)


def _rope_tables(max_len, head_dim):
    inv = ROPE_THETA ** (-jnp.arange(0, head_dim, 2, dtype=F32) / head_dim)
    ang = jnp.arange(max_len, dtype=F32)[:, None] * inv[None, :]
    ang = jnp.concatenate([ang, ang], -1)
    cos = jnp.cos(ang)
    sin = jnp.sin(ang)
    half = head_dim // 2
    sign = jnp.where(jnp.arange(head_dim) < half, -1.0, 1.0).astype(F32)
    sin = sin * sign[None, :]
    reps = LANES // head_dim
    return jnp.tile(cos, (1, reps)), jnp.tile(sin, (1, reps))


def _rope_lane_tile(y, cos, sin_signed, head_dim):
    half = head_dim // 2
    lane = lax.broadcasted_iota(jnp.int32, y.shape, 1)
    lo = (lane & (head_dim - 1)) < half
    fwd = pltpu.roll(y, half, 1)
    bwd = pltpu.roll(y, LANES - half, 1)
    return y * cos + jnp.where(lo, bwd, fwd) * sin_signed


PROJ_ROW_SUBTILE = 512


def _proj_rope_kernel(x_ref, w_ref, cos_ref, sin_ref, o_ref, *, head_dim, n_scaled, scale):
    j = pl.program_id(1)
    tm = x_ref.shape[0]
    tr = min(PROJ_ROW_SUBTILE, tm)
    tiles_per_block = o_ref.shape[1] // LANES
    for r0 in range(0, tm, tr):
        y = jnp.dot(x_ref[r0:r0 + tr, :].astype(BF16), w_ref[...], preferred_element_type=F32)
        cos = cos_ref[r0:r0 + tr, :]
        sin = sin_ref[r0:r0 + tr, :]
        for c in range(tiles_per_block):
            r = _rope_lane_tile(y[:, c * LANES:(c + 1) * LANES], cos, sin, head_dim)
            tile_idx = j * tiles_per_block + c
            r = r * jnp.where(tile_idx < n_scaled, scale, 1.0).astype(F32)
            o_ref[r0:r0 + tr, c * LANES:(c + 1) * LANES] = r.astype(o_ref.dtype)


def _proj_rope(x, w, cos, sin, *, head_dim, n_scaled, scale, tm, tn, seq_len):
    t, d = x.shape
    n = w.shape[1]
    pos_spec = pl.BlockSpec((tm, LANES), lambda i, j: (i % (seq_len // tm), 0))
    return pl.pallas_call(
        functools.partial(_proj_rope_kernel, head_dim=head_dim, n_scaled=n_scaled, scale=scale),
        grid=(t // tm, n // tn),
        in_specs=[
            pl.BlockSpec((tm, d), lambda i, j: (i, 0)),
            pl.BlockSpec((d, tn), lambda i, j: (0, j)),
            pos_spec,
            pos_spec,
        ],
        out_specs=pl.BlockSpec((tm, tn), lambda i, j: (i, j)),
        out_shape=jax.ShapeDtypeStruct((t, n), BF16),
        compiler_params=_cparams(2),
        name="proj_rope",
    )(x, w, cos, sin)


def _proj_t_kernel(x_ref, wt_ref, o_ref, *, tk):
    yt = lax.dot_general(wt_ref[...], x_ref[...].astype(BF16), (((1,), (1,)), ((), ())),
                         preferred_element_type=F32)
    for c in range(o_ref.shape[0]):
        o_ref[c] = yt[:, c * tk:(c + 1) * tk].astype(o_ref.dtype)


def _proj_tr_kernel(x_ref, w_ref, o_ref, y_ref, *, tk):
    y_ref[...] = jnp.dot(x_ref[...].astype(BF16), w_ref[...], preferred_element_type=F32)
    for c in range(o_ref.shape[0]):
        o_ref[c] = y_ref[c * tk:(c + 1) * tk, :].T.astype(o_ref.dtype)


def _proj_t(x, w, *, tm, tk, weight_is_transposed):
    t, d = x.shape
    n = w.shape[0] if weight_is_transposed else w.shape[1]
    assert tm % tk == 0
    kernel = _proj_t_kernel if weight_is_transposed else _proj_tr_kernel
    return pl.pallas_call(
        functools.partial(kernel, tk=tk),
        grid=(t // tm,),
        in_specs=[
            pl.BlockSpec((tm, d), lambda i: (i, 0)),
            pl.BlockSpec(w.shape, lambda i: (0, 0)),
        ],
        out_specs=pl.BlockSpec((tm // tk, n, tk), lambda i: (i, 0, 0)),
        out_shape=jax.ShapeDtypeStruct((t // tk, n, tk), BF16),
        scratch_shapes=[] if weight_is_transposed else [pltpu.VMEM((tm, n), F32)],
        compiler_params=_cparams(1),
        name="proj_t",
    )(x, w)


ONES_ROWS = 16
DA_CHUNKS_PER_BODY = 4
DA_BODY_UNROLL = 1


def _diff_attn_kernel(q_ref, k_ref, vt_ref, lq1_ref, lk1_ref, lq2_ref, lk2_ref, g_ref, o_ref,
                      qb_ref, s_ref, cmax_ref, p_ref, a_ref, m_ref, acc_ref,
                      *, tq, tk, nk, group, hd, dv, lam_init):
    q = q_ref[...]
    lane = lax.broadcasted_iota(jnp.int32, q.shape, 1)
    zero = jnp.zeros_like(q)
    qb_ref[0:tq, :] = jnp.where(lane < hd, q, zero)
    qb_ref[tq:2 * tq, :] = jnp.where(lane >= hd, q, zero)
    ones = jnp.ones((ONES_ROWS, tk), BF16)

    def scores(i, slot):
        off = pl.multiple_of(i * tk, tk)
        k = k_ref[pl.ds(off, tk), :]
        s = lax.dot_general(k, qb_ref[...], (((1,), (1,)), ((), ())),
                            preferred_element_type=F32)
        s_ref[slot] = s
        cmax_ref[slot] = jnp.max(s, axis=0, keepdims=True)

    def softmax(slot):
        for c in range(2):
            m_old = m_ref[c]
            m_new = jnp.maximum(m_old, cmax_ref[slot, :, c * tq:(c + 1) * tq])
            a_ref[slot, c] = jnp.exp2(m_old - m_new)
            p_ref[slot, c] = jnp.exp2(s_ref[slot, :, c * tq:(c + 1) * tq] - m_new).astype(BF16)
            m_ref[c] = m_new

    def values(i, slot):
        vt = jnp.concatenate([vt_ref[i], ones], axis=0)
        for c in range(2):
            acc_ref[c] = a_ref[slot, c] * acc_ref[c] + jnp.dot(vt, p_ref[slot, c],
                                                               preferred_element_type=F32)

    m_ref[...] = jnp.full(m_ref.shape, -jnp.inf, F32)
    acc_ref[...] = jnp.zeros(acc_ref.shape, F32)

    def body(j, with_values=True, with_scores=True):
        if with_values:
            for g in range(group):
                values((j - 1) * group + g, g)
        for g in range(group):
            softmax(g)
        if with_scores:
            for g in range(group):
                scores((j + 1) * group + g, g)

    n_body = nk // group
    for g in range(group):
        scores(g, g)
    body(0, with_values=False)

    def loop_body(j, carry):
        body(j)
        return carry

    lax.fori_loop(1, n_body - 1, loop_body, 0, unroll=DA_BODY_UNROLL)
    body(n_body - 1, with_scores=False)
    for g in range(group):
        values(nk - group + g, g)

    lam = (jnp.exp(jnp.sum(lq1_ref[...] * lk1_ref[...], axis=-1, keepdims=True))
           - jnp.exp(jnp.sum(lq2_ref[...] * lk2_ref[...], axis=-1, keepdims=True)) + lam_init)
    o = (acc_ref[0, 0:dv] * (1.0 / acc_ref[0, dv:dv + 1])
         - lam * (acc_ref[1, 0:dv] * (1.0 / acc_ref[1, dv:dv + 1])))
    ms = jnp.mean(o * o, axis=0, keepdims=True)
    o = o * lax.rsqrt(ms + LN_EPS) * g_ref[...] * (1.0 - lam_init)
    o_ref[...] = o.T.astype(o_ref.dtype)


def _diff_attn(qk, vt, lq1, lk1, lq2, lk2, g_col, *, seq_len, n_heads, tq, tk, lam_init):
    dv = g_col.shape[0]
    hd = lq1.shape[1]
    n_seq = qk.shape[0] // seq_len
    nq = seq_len // tq
    nk = seq_len // tk
    group = min(DA_CHUNKS_PER_BODY, nk // 2)
    assert nk % group == 0 and nk // group >= 2, "the chunk loop is software-pipelined in groups"
    kernel = functools.partial(_diff_attn_kernel, tq=tq, tk=tk, nk=nk, group=group, hd=hd, dv=dv,
                               lam_init=lam_init)
    vec = pl.BlockSpec((1, hd), lambda b, h, i: (0, 0))
    return pl.pallas_call(
        kernel,
        grid=(n_seq, n_heads, nq),
        in_specs=[
            pl.BlockSpec((tq, 2 * hd), lambda b, h, i: (b * nq + i, h)),
            pl.BlockSpec((seq_len, 2 * hd), lambda b, h, i: (b, n_heads + h)),
            pl.BlockSpec((nk, dv, tk), lambda b, h, i: (b, h, 0)),
            vec, vec, vec, vec,
            pl.BlockSpec((dv, 1), lambda b, h, i: (0, 0)),
        ],
        out_specs=pl.BlockSpec((tq, dv), lambda b, h, i: (b * nq + i, h)),
        out_shape=jax.ShapeDtypeStruct((qk.shape[0], n_heads * dv), BF16),
        scratch_shapes=[
            pltpu.VMEM((2 * tq, 2 * hd), BF16),
            pltpu.VMEM((group, tk, 2 * tq), F32),
            pltpu.VMEM((group, 1, 2 * tq), F32),
            pltpu.VMEM((group, 2, tk, tq), BF16),
            pltpu.VMEM((group, 2, 1, tq), F32),
            pltpu.VMEM((2, 1, tq), F32),
            pltpu.VMEM((2, dv + ONES_ROWS, tq), F32),
        ],
        compiler_params=_cparams(3),
        name="diff_attn",
    )(qk, qk, vt, lq1, lk1, lq2, lk2, g_col)


def _window_attn_kernel(sink_ref, q_ref, k_ref, kp_ref, kn_ref, vt_ref, vtp_ref, vtn_ref, o_ref,
                        kwin_ref, *, tq, n_kv, group, hd, seq_len):
    w = WG_WINDOW
    nb = tq // w
    pos0 = lax.rem(pl.program_id(0) * tq, jnp.int32(seq_len))

    kwin_ref[0:w, :] = kp_ref[...]
    kwin_ref[w:w + tq, :] = k_ref[...]
    kwin_ref[w + tq:w + tq + w, :] = kn_ref[...]

    def vt_block(j, g):
        ref, idx = (vtp_ref, 0) if j < 0 else (vtn_ref, 0) if j >= nb else (vt_ref, j)
        return ref[idx, g * hd:(g + 1) * hd, :]

    cols = group * w
    c_idx = lax.broadcasted_iota(jnp.int32, (3 * w, cols), 0)
    a_idx = lax.broadcasted_iota(jnp.int32, (3 * w, cols), 1) & (w - 1)
    rel = c_idx - a_idx
    band = (rel >= 0) & (rel <= 2 * w)
    head_in_group = lax.broadcasted_iota(jnp.int32, (1, cols), 1) // w
    ones = jnp.ones((ONES_ROWS, 3 * w), BF16)
    log2e = math.log2(math.e)

    key_row = lax.broadcasted_iota(jnp.int32, (3 * w, 1), 0)

    for n in range(nb):
        kpos = pos0 + (n - 1) * w + key_row
        valid = band & ((kpos >= 0) & (kpos < seq_len))
        for g in range(n_kv):
            kw = kwin_ref[n * w:n * w + 3 * w, g * hd:(g + 1) * hd]
            qg = jnp.concatenate(
                [q_ref[n * w:(n + 1) * w, (g * group + j) * hd:(g * group + j + 1) * hd]
                 for j in range(group)], axis=0)
            s = lax.dot_general(kw, qg, (((1,), (1,)), ((), ())), preferred_element_type=F32)
            s = jnp.where(valid, s, -jnp.inf)
            sk = jnp.zeros((1, cols), F32)
            for j in range(group):
                sk = jnp.where(head_in_group == j, sink_ref[g * group + j] * log2e, sk)
            m = jnp.maximum(jnp.max(s, axis=0, keepdims=True), sk)
            p = jnp.exp2(s - m).astype(BF16)
            vt = jnp.concatenate([vt_block(n - 1, g), vt_block(n, g), vt_block(n + 1, g)], axis=1)
            ot = jnp.dot(jnp.concatenate([vt, ones], axis=0), p, preferred_element_type=F32)
            den = ot[hd:hd + 1, :] + jnp.exp2(sk - m)
            o = (ot[0:hd, :] * (1.0 / den)).T
            for j in range(group):
                h = g * group + j
                o_ref[n * w:(n + 1) * w, h * hd:(h + 1) * hd] = o[j * w:(j + 1) * w, :].astype(o_ref.dtype)


def _window_attn(q, k, vt, sink, *, n_heads, n_kv, tq, seq_len):
    t, qd = q.shape
    hd = qd // n_heads
    kd = n_kv * hd
    w = WG_WINDOW
    r = tq // w
    n_blk = t // w
    kernel = functools.partial(_window_attn_kernel, tq=tq, n_kv=n_kv, group=n_heads // n_kv, hd=hd,
                               seq_len=seq_len)
    prev_blk = lambda i: jnp.maximum(i * r - 1, 0)
    next_blk = lambda i: jnp.minimum((i + 1) * r, n_blk - 1)
    return pl.pallas_call(
        kernel,
        grid=(t // tq,),
        in_specs=[
            pl.BlockSpec(memory_space=pltpu.SMEM),
            pl.BlockSpec((tq, qd), lambda i: (i, 0)),
            pl.BlockSpec((tq, kd), lambda i: (i, 0)),
            pl.BlockSpec((w, kd), lambda i: (prev_blk(i), 0)),
            pl.BlockSpec((w, kd), lambda i: (next_blk(i), 0)),
            pl.BlockSpec((r, kd, w), lambda i: (i, 0, 0)),
            pl.BlockSpec((1, kd, w), lambda i: (prev_blk(i), 0, 0)),
            pl.BlockSpec((1, kd, w), lambda i: (next_blk(i), 0, 0)),
        ],
        out_specs=pl.BlockSpec((tq, qd), lambda i: (i, 0)),
        scratch_shapes=[pltpu.VMEM((tq + 2 * w, kd), BF16)],
        out_shape=jax.ShapeDtypeStruct((t, qd), BF16),
        compiler_params=_cparams(1),
        name="window_attn",
    )(sink, q, k, k, k, vt, vt, vt)


def _outproj_kernel(x_ref, o_ref, w_ref, g_ref, b_ref, y_ref, *, alpha):
    h = jnp.dot(o_ref[...], w_ref[...], preferred_element_type=F32)
    y_ref[...] = _layernorm_rows(alpha * x_ref[...] + h, g_ref[...], b_ref[...])


def _outproj(x, o, w, g, b, *, alpha, tm):
    t, d = x.shape
    k = o.shape[1]
    return pl.pallas_call(
        functools.partial(_outproj_kernel, alpha=alpha),
        grid=(t // tm,),
        in_specs=[
            pl.BlockSpec((tm, d), lambda i: (i, 0)),
            pl.BlockSpec((tm, k), lambda i: (i, 0)),
            pl.BlockSpec((k, d), lambda i: (0, 0)),
            pl.BlockSpec((1, d), lambda i: (0, 0)),
            pl.BlockSpec((1, d), lambda i: (0, 0)),
        ],
        out_specs=pl.BlockSpec((tm, d), lambda i: (i, 0)),
        out_shape=jax.ShapeDtypeStruct((t, d), F32),
        compiler_params=_cparams(1),
        name="outproj_ln",
    )(x, o, w, g, b)


def _largest_divisor(n, cap, multiple):
    best = None
    for c in range(multiple, min(n, cap) + 1, multiple):
        if n % c == 0:
            best = c
    assert best is not None, (n, cap, multiple)
    return best


def _tiles(seq_len, d_ff):
    return dict(
        tm=_largest_divisor(seq_len, 512, LANES),
        tmp=_largest_divisor(seq_len, 1024, LANES),
        tmf=_largest_divisor(seq_len, 1024, LANES),
        tf=_largest_divisor(d_ff, 512, 2 * LANES),
        tq=_largest_divisor(seq_len, 1024, LANES),
        tk=_largest_divisor(seq_len // 2, 512, LANES),
        tw=_largest_divisor(seq_len, 512, WG_WINDOW),
    )


def kernel(x_prompt, x_sample, ffn1_w_gate, ffn1_w_up, ffn1_w_down, ffn2_w_gate, ffn2_w_up, ffn2_w_down, ln_g, ln_b, da_w_qkv, da_w_o, da_lambda_q1, da_lambda_k1, da_lambda_q2, da_lambda_k2, da_subln_g, wg_w_qkv, wg_w_o, wg_sink):
    bp, sp, d = x_prompt.shape
    bs, ss, _ = x_sample.shape
    depth = ffn1_w_gate.shape[0]
    d_ff = ffn1_w_gate.shape[2]
    alpha = (2.0 * depth) ** 0.25

    da_hd = da_lambda_q1.shape[1]
    da_dv = da_subln_g.shape[1]
    da_heads = d // da_dv
    da_qd = 2 * da_heads * da_hd
    wg_heads = wg_sink.shape[1]
    wg_hd = d // wg_heads
    wg_kv = (wg_w_qkv.shape[2] - d) // (2 * wg_hd)
    assert 2 * da_hd == LANES and da_dv == LANES and wg_hd == LANES

    cos_a, sin_a = _rope_tables(max(sp, ss), da_hd)
    cos_b, sin_b = _rope_tables(max(sp, ss), wg_hd)

    ffn_w = [tuple(w.astype(BF16) for w in ws)
             for ws in ((ffn1_w_gate, ffn1_w_up, ffn1_w_down), (ffn2_w_gate, ffn2_w_up, ffn2_w_down))]
    mix_w = []
    for i in range(depth):
        j = i // 2
        if i % 2 == 0:
            w = da_w_qkv[j]
            mix_w.append((w[:, :2 * da_qd].astype(BF16), w[:, 2 * da_qd:].T.astype(BF16), da_w_o[j].astype(BF16)))
        else:
            w = wg_w_qkv[j].astype(BF16)
            kd = wg_kv * wg_hd
            mix_w.append((w[:, :d], w[:, d:d + kd], w[:, d + kd:], wg_w_o[j].astype(BF16)))

    def ln(i, k):
        return ln_g[i, k][None, :], ln_b[i, k][None, :]

    def trunk(x, seq_len):
        tl = _tiles(seq_len, d_ff)
        tm, tmp, tmf, tf, tq, tk, tw = (tl[n] for n in ("tm", "tmp", "tmf", "tf", "tq", "tk", "tw"))
        for i in range(depth):
            x = _ffn(x, *ffn_w[0], *ln(i, 0), layer=i, alpha=alpha, tm=tmf, tf=tf)
            j = i // 2
            if i % 2 == 0:
                lam_init = 0.8 - 0.6 * math.exp(-0.3 * i)
                w_qk, w_vt, w_o = mix_w[i]
                qk = _proj_rope(x, w_qk, cos_a, sin_a, head_dim=da_hd, n_scaled=da_qd // LANES,
                                scale=da_hd ** -0.5 * math.log2(math.e), tm=tmp, tn=da_qd // 2,
                                seq_len=seq_len)
                vt = _proj_t(x, w_vt, tm=tm, tk=tk, weight_is_transposed=True)
                o = _diff_attn(qk, vt, da_lambda_q1[j][None, :], da_lambda_k1[j][None, :],
                               da_lambda_q2[j][None, :], da_lambda_k2[j][None, :], da_subln_g[j][:, None],
                               seq_len=seq_len, n_heads=da_heads, tq=tq, tk=tk, lam_init=lam_init)
            else:
                w_q, w_k, w_v, w_o = mix_w[i]
                q = _proj_rope(x, w_q, cos_b, sin_b, head_dim=wg_hd, n_scaled=d // LANES,
                               scale=wg_hd ** -0.5 * math.log2(math.e), tm=tmp, tn=d // 2, seq_len=seq_len)
                k = _proj_rope(x, w_k, cos_b, sin_b, head_dim=wg_hd, n_scaled=0, scale=1.0, tm=tmp,
                               tn=wg_kv * wg_hd, seq_len=seq_len)
                vt = _proj_t(x, w_v, tm=tmp, tk=WG_WINDOW, weight_is_transposed=False)
                o = _window_attn(q, k, vt, wg_sink[j], n_heads=wg_heads, n_kv=wg_kv, tq=tw, seq_len=seq_len)
            x = _outproj(x, o, w_o, *ln(i, 1), alpha=alpha, tm=tm)
            x = _ffn(x, *ffn_w[1], *ln(i, 2), layer=i, alpha=alpha, tm=tmf, tf=tf)
        return x

    y_prompt = trunk(x_prompt.reshape(bp * sp, d), sp).reshape(bp, sp, d)
    y_sample = trunk(x_sample.reshape(bs * ss, d), ss).reshape(bs, ss, d)
    return (y_prompt, y_sample)
```

```python
import functools
import math

import jax
import jax.numpy as jnp
from jax import lax
from jax.experimental import pallas as pl
from jax.experimental.pallas import tpu as pltpu

F32 = jnp.float32
BF16 = jnp.bfloat16

LN_EPS = 1e-5
ROPE_THETA = 10000.0
WG_WINDOW = 128
LANES = 128
V7X_VMEM_LIMIT_BYTES = 60000 * 1024


def _cparams(n_axes):
    return pltpu.CompilerParams(
        dimension_semantics=("arbitrary",) * n_axes,
        vmem_limit_bytes=V7X_VMEM_LIMIT_BYTES,
    )


def _layernorm_rows(y, g, b):
    mu = jnp.mean(y, axis=-1, keepdims=True)
    yc = y - mu
    var = jnp.mean(yc * yc, axis=-1, keepdims=True)
    return yc * lax.rsqrt(var + LN_EPS) * g + b


FFN_ROW_SUBTILE = 512
LN_ROW_SUBTILE = 128


def _ffn_kernel(x_ref, wg_ref, wu_ref, wd_ref, g_ref, b_ref, o_ref, xb_ref, *, alpha, nf):
    f = pl.program_id(1)
    tm = xb_ref.shape[0]
    tr = min(FFN_ROW_SUBTILE, tm)
    tl = min(LN_ROW_SUBTILE, tr)

    def step(first, last):
        for r in range(0, tm, tr):
            if first:
                xb_ref[r:r + tr, :] = x_ref[r:r + tr, :].astype(BF16)
            xb = xb_ref[r:r + tr, :]
            hg = jnp.dot(xb, wg_ref[...], preferred_element_type=F32)
            hu = jnp.dot(xb, wu_ref[...], preferred_element_type=F32)
            a = (hg * jax.nn.sigmoid(hg) * hu).astype(BF16)
            part = jnp.dot(a, wd_ref[...], preferred_element_type=F32)
            if first:
                o_ref[r:r + tr, :] = part
            else:
                o_ref[r:r + tr, :] += part
            if last:
                for q in range(r, r + tr, tl):
                    y = alpha * x_ref[q:q + tl, :] + 0.5 * o_ref[q:q + tl, :]
                    o_ref[q:q + tl, :] = _layernorm_rows(y, g_ref[...], b_ref[...])

    if nf == 1:
        step(True, True)
    else:
        pl.when(f == 0)(lambda: step(True, False))
        pl.when((f > 0) & (f < nf - 1))(lambda: step(False, False))
        pl.when(f == nf - 1)(lambda: step(False, True))


def _ffn(x, wg, wu, wd, g, b, *, layer, alpha, tm, tf):
    t, d = x.shape
    f = wg.shape[2]
    nf = f // tf
    return pl.pallas_call(
        functools.partial(_ffn_kernel, alpha=alpha, nf=nf),
        grid=(t // tm, nf),
        in_specs=[
            pl.BlockSpec((tm, d), lambda i, j: (i, 0)),
            pl.BlockSpec((None, d, tf), lambda i, j: (layer, 0, j)),
            pl.BlockSpec((None, d, tf), lambda i, j: (layer, 0, j)),
            pl.BlockSpec((None, tf, d), lambda i, j: (layer, j, 0)),
            pl.BlockSpec((1, d), lambda i, j: (0, 0)),
            pl.BlockSpec((1, d), lambda i, j: (0, 0)),
        ],
        out_specs=pl.BlockSpec((tm, d), lambda i, j: (i, 0)),
        out_shape=jax.ShapeDtypeStruct((t, d), F32),
        scratch_shapes=[pltpu.VMEM((tm, d), BF16)],
        compiler_params=_cparams(2),
        name="ffn_ln",
    )(x, wg, wu, wd, g, b)


def _rope_tables(max_len, head_dim):
    inv = ROPE_THETA ** (-jnp.arange(0, head_dim, 2, dtype=F32) / head_dim)
    ang = jnp.arange(max_len, dtype=F32)[:, None] * inv[None, :]
    ang = jnp.concatenate([ang, ang], -1)
    cos = jnp.cos(ang)
    sin = jnp.sin(ang)
    half = head_dim // 2
    sign = jnp.where(jnp.arange(head_dim) < half, -1.0, 1.0).astype(F32)
    sin = sin * sign[None, :]
    reps = LANES // head_dim
    return jnp.tile(cos, (1, reps)), jnp.tile(sin, (1, reps))


def _rope_lane_tile(y, cos, sin_signed, head_dim):
    half = head_dim // 2
    lane = lax.broadcasted_iota(jnp.int32, y.shape, 1)
    lo = (lane & (head_dim - 1)) < half
    fwd = pltpu.roll(y, half, 1)
    bwd = pltpu.roll(y, LANES - half, 1)
    return y * cos + jnp.where(lo, bwd, fwd) * sin_signed


PROJ_ROW_SUBTILE = 512


def _proj_rope_kernel(x_ref, w_ref, cos_ref, sin_ref, o_ref, *, head_dim, n_scaled, scale):
    j = pl.program_id(1)
    tm = x_ref.shape[0]
    tr = min(PROJ_ROW_SUBTILE, tm)
    tiles_per_block = o_ref.shape[1] // LANES
    for r0 in range(0, tm, tr):
        y = jnp.dot(x_ref[r0:r0 + tr, :].astype(BF16), w_ref[...], preferred_element_type=F32)
        cos = cos_ref[r0:r0 + tr, :]
        sin = sin_ref[r0:r0 + tr, :]
        for c in range(tiles_per_block):
            r = _rope_lane_tile(y[:, c * LANES:(c + 1) * LANES], cos, sin, head_dim)
            tile_idx = j * tiles_per_block + c
            r = r * jnp.where(tile_idx < n_scaled, scale, 1.0).astype(F32)
            o_ref[r0:r0 + tr, c * LANES:(c + 1) * LANES] = r.astype(o_ref.dtype)


def _proj_rope(x, w, cos, sin, *, head_dim, n_scaled, scale, tm, tn, seq_len):
    t, d = x.shape
    n = w.shape[1]
    pos_spec = pl.BlockSpec((tm, LANES), lambda i, j: (i % (seq_len // tm), 0))
    return pl.pallas_call(
        functools.partial(_proj_rope_kernel, head_dim=head_dim, n_scaled=n_scaled, scale=scale),
        grid=(t // tm, n // tn),
        in_specs=[
            pl.BlockSpec((tm, d), lambda i, j: (i, 0)),
            pl.BlockSpec((d, tn), lambda i, j: (0, j)),
            pos_spec,
            pos_spec,
        ],
        out_specs=pl.BlockSpec((tm, tn), lambda i, j: (i, j)),
        out_shape=jax.ShapeDtypeStruct((t, n), BF16),
        compiler_params=_cparams(2),
        name="proj_rope",
    )(x, w, cos, sin)


def _proj_t_kernel(x_ref, wt_ref, o_ref, *, tk):
    yt = lax.dot_general(wt_ref[...], x_ref[...].astype(BF16), (((1,), (1,)), ((), ())),
                         preferred_element_type=F32)
    for c in range(o_ref.shape[0]):
        o_ref[c] = yt[:, c * tk:(c + 1) * tk].astype(o_ref.dtype)


def _proj_tr_kernel(x_ref, w_ref, o_ref, y_ref, *, tk):
    y_ref[...] = jnp.dot(x_ref[...].astype(BF16), w_ref[...], preferred_element_type=F32)
    for c in range(o_ref.shape[0]):
        o_ref[c] = y_ref[c * tk:(c + 1) * tk, :].T.astype(o_ref.dtype)


def _proj_t(x, w, *, tm, tk, weight_is_transposed):
    t, d = x.shape
    n = w.shape[0] if weight_is_transposed else w.shape[1]
    assert tm % tk == 0
    kernel = _proj_t_kernel if weight_is_transposed else _proj_tr_kernel
    return pl.pallas_call(
        functools.partial(kernel, tk=tk),
        grid=(t // tm,),
        in_specs=[
            pl.BlockSpec((tm, d), lambda i: (i, 0)),
            pl.BlockSpec(w.shape, lambda i: (0, 0)),
        ],
        out_specs=pl.BlockSpec((tm // tk, n, tk), lambda i: (i, 0, 0)),
        out_shape=jax.ShapeDtypeStruct((t // tk, n, tk), BF16),
        scratch_shapes=[] if weight_is_transposed else [pltpu.VMEM((tm, n), F32)],
        compiler_params=_cparams(1),
        name="proj_t",
    )(x, w)


ONES_ROWS = 16
DA_CHUNKS_PER_BODY = 4


def _diff_attn_kernel(q_ref, k_ref, vt_ref, lq1_ref, lk1_ref, lq2_ref, lk2_ref, g_ref, o_ref,
                      qb_ref, s_ref, cmax_ref, p_ref, a_ref, m_ref, acc_ref,
                      *, tq, tk, nk, group, hd, dv, lam_init):
    q = q_ref[...]
    lane = lax.broadcasted_iota(jnp.int32, q.shape, 1)
    zero = jnp.zeros_like(q)
    qb_ref[0:tq, :] = jnp.where(lane < hd, q, zero)
    qb_ref[tq:2 * tq, :] = jnp.where(lane >= hd, q, zero)
    ones = jnp.ones((ONES_ROWS, tk), BF16)

    def scores(i, slot):
        off = pl.multiple_of(i * tk, tk)
        k = k_ref[pl.ds(off, tk), :]
        s = lax.dot_general(k, qb_ref[...], (((1,), (1,)), ((), ())),
                            preferred_element_type=F32)
        s_ref[slot] = s
        cmax_ref[slot] = jnp.max(s, axis=0, keepdims=True)

    def softmax(slot):
        for c in range(2):
            m_old = m_ref[c]
            m_new = jnp.maximum(m_old, cmax_ref[slot, :, c * tq:(c + 1) * tq])
            a_ref[slot, c] = jnp.exp2(m_old - m_new)
            p_ref[slot, c] = jnp.exp2(s_ref[slot, :, c * tq:(c + 1) * tq] - m_new).astype(BF16)
            m_ref[c] = m_new

    def values(i, slot):
        vt = jnp.concatenate([vt_ref[i], ones], axis=0)
        for c in range(2):
            acc_ref[c] = a_ref[slot, c] * acc_ref[c] + jnp.dot(vt, p_ref[slot, c],
                                                               preferred_element_type=F32)

    m_ref[...] = jnp.full(m_ref.shape, -jnp.inf, F32)
    acc_ref[...] = jnp.zeros(acc_ref.shape, F32)

    def body(j, with_values=True, with_scores=True):
        if with_values:
            for g in range(group):
                values((j - 1) * group + g, g)
        for g in range(group):
            softmax(g)
        if with_scores:
            for g in range(group):
                scores((j + 1) * group + g, g)

    n_body = nk // group
    for g in range(group):
        scores(g, g)
    body(0, with_values=False)

    def loop_body(j, carry):
        body(j)
        return carry

    lax.fori_loop(1, n_body - 1, loop_body, 0)
    body(n_body - 1, with_scores=False)
    for g in range(group):
        values(nk - group + g, g)

    lam = (jnp.exp(jnp.sum(lq1_ref[...] * lk1_ref[...], axis=-1, keepdims=True))
           - jnp.exp(jnp.sum(lq2_ref[...] * lk2_ref[...], axis=-1, keepdims=True)) + lam_init)
    o = (acc_ref[0, 0:dv] * (1.0 / acc_ref[0, dv:dv + 1])
         - lam * (acc_ref[1, 0:dv] * (1.0 / acc_ref[1, dv:dv + 1])))
    ms = jnp.mean(o * o, axis=0, keepdims=True)
    o = o * lax.rsqrt(ms + LN_EPS) * g_ref[...] * (1.0 - lam_init)
    o_ref[...] = o.T.astype(o_ref.dtype)


def _diff_attn(qk, vt, lq1, lk1, lq2, lk2, g_col, *, seq_len, n_heads, tq, tk, lam_init):
    dv = g_col.shape[0]
    hd = lq1.shape[1]
    n_seq = qk.shape[0] // seq_len
    nq = seq_len // tq
    nk = seq_len // tk
    group = min(DA_CHUNKS_PER_BODY, nk // 2)
    assert nk % group == 0 and nk // group >= 2, "the chunk loop is software-pipelined in groups"
    kernel = functools.partial(_diff_attn_kernel, tq=tq, tk=tk, nk=nk, group=group, hd=hd, dv=dv,
                               lam_init=lam_init)
    vec = pl.BlockSpec((1, hd), lambda b, h, i: (0, 0))
    return pl.pallas_call(
        kernel,
        grid=(n_seq, n_heads, nq),
        in_specs=[
            pl.BlockSpec((tq, 2 * hd), lambda b, h, i: (b * nq + i, h)),
            pl.BlockSpec((seq_len, 2 * hd), lambda b, h, i: (b, n_heads + h)),
            pl.BlockSpec((nk, dv, tk), lambda b, h, i: (b, h, 0)),
            vec, vec, vec, vec,
            pl.BlockSpec((dv, 1), lambda b, h, i: (0, 0)),
        ],
        out_specs=pl.BlockSpec((tq, dv), lambda b, h, i: (b * nq + i, h)),
        out_shape=jax.ShapeDtypeStruct((qk.shape[0], n_heads * dv), BF16),
        scratch_shapes=[
            pltpu.VMEM((2 * tq, 2 * hd), BF16),
            pltpu.VMEM((group, tk, 2 * tq), F32),
            pltpu.VMEM((group, 1, 2 * tq), F32),
            pltpu.VMEM((group, 2, tk, tq), BF16),
            pltpu.VMEM((group, 2, 1, tq), F32),
            pltpu.VMEM((2, 1, tq), F32),
            pltpu.VMEM((2, dv + ONES_ROWS, tq), F32),
        ],
        compiler_params=_cparams(3),
        name="diff_attn",
    )(qk, qk, vt, lq1, lk1, lq2, lk2, g_col)


def _window_attn_kernel(sink_ref, q_ref, k_ref, kp_ref, kn_ref, vt_ref, vtp_ref, vtn_ref, o_ref,
                        kwin_ref, *, tq, n_kv, group, hd, seq_len):
    w = WG_WINDOW
    nb = tq // w
    pos0 = lax.rem(pl.program_id(0) * tq, jnp.int32(seq_len))

    kwin_ref[0:w, :] = kp_ref[...]
    kwin_ref[w:w + tq, :] = k_ref[...]
    kwin_ref[w + tq:w + tq + w, :] = kn_ref[...]

    def vt_block(j, g):
        ref, idx = (vtp_ref, 0) if j < 0 else (vtn_ref, 0) if j >= nb else (vt_ref, j)
        return ref[idx, g * hd:(g + 1) * hd, :]

    cols = group * w
    c_idx = lax.broadcasted_iota(jnp.int32, (3 * w, cols), 0)
    a_idx = lax.broadcasted_iota(jnp.int32, (3 * w, cols), 1) & (w - 1)
    rel = c_idx - a_idx
    band = (rel >= 0) & (rel <= 2 * w)
    head_in_group = lax.broadcasted_iota(jnp.int32, (1, cols), 1) // w
    ones = jnp.ones((ONES_ROWS, 3 * w), BF16)
    log2e = math.log2(math.e)

    key_row = lax.broadcasted_iota(jnp.int32, (3 * w, 1), 0)

    for n in range(nb):
        kpos = pos0 + (n - 1) * w + key_row
        valid = band & ((kpos >= 0) & (kpos < seq_len))
        for g in range(n_kv):
            kw = kwin_ref[n * w:n * w + 3 * w, g * hd:(g + 1) * hd]
            qg = jnp.concatenate(
                [q_ref[n * w:(n + 1) * w, (g * group + j) * hd:(g * group + j + 1) * hd]
                 for j in range(group)], axis=0)
            s = lax.dot_general(kw, qg, (((1,), (1,)), ((), ())), preferred_element_type=F32)
            s = jnp.where(valid, s, -jnp.inf)
            sk = jnp.zeros((1, cols), F32)
            for j in range(group):
                sk = jnp.where(head_in_group == j, sink_ref[g * group + j] * log2e, sk)
            m = jnp.maximum(jnp.max(s, axis=0, keepdims=True), sk)
            p = jnp.exp2(s - m).astype(BF16)
            vt = jnp.concatenate([vt_block(n - 1, g), vt_block(n, g), vt_block(n + 1, g)], axis=1)
            ot = jnp.dot(jnp.concatenate([vt, ones], axis=0), p, preferred_element_type=F32)
            den = ot[hd:hd + 1, :] + jnp.exp2(sk - m)
            o = (ot[0:hd, :] * (1.0 / den)).T
            for j in range(group):
                h = g * group + j
                o_ref[n * w:(n + 1) * w, h * hd:(h + 1) * hd] = o[j * w:(j + 1) * w, :].astype(o_ref.dtype)


def _window_attn(q, k, vt, sink, *, n_heads, n_kv, tq, seq_len):
    t, qd = q.shape
    hd = qd // n_heads
    kd = n_kv * hd
    w = WG_WINDOW
    r = tq // w
    n_blk = t // w
    kernel = functools.partial(_window_attn_kernel, tq=tq, n_kv=n_kv, group=n_heads // n_kv, hd=hd,
                               seq_len=seq_len)
    prev_blk = lambda i: jnp.maximum(i * r - 1, 0)
    next_blk = lambda i: jnp.minimum((i + 1) * r, n_blk - 1)
    return pl.pallas_call(
        kernel,
        grid=(t // tq,),
        in_specs=[
            pl.BlockSpec(memory_space=pltpu.SMEM),
            pl.BlockSpec((tq, qd), lambda i: (i, 0)),
            pl.BlockSpec((tq, kd), lambda i: (i, 0)),
            pl.BlockSpec((w, kd), lambda i: (prev_blk(i), 0)),
            pl.BlockSpec((w, kd), lambda i: (next_blk(i), 0)),
            pl.BlockSpec((r, kd, w), lambda i: (i, 0, 0)),
            pl.BlockSpec((1, kd, w), lambda i: (prev_blk(i), 0, 0)),
            pl.BlockSpec((1, kd, w), lambda i: (next_blk(i), 0, 0)),
        ],
        out_specs=pl.BlockSpec((tq, qd), lambda i: (i, 0)),
        scratch_shapes=[pltpu.VMEM((tq + 2 * w, kd), BF16)],
        out_shape=jax.ShapeDtypeStruct((t, qd), BF16),
        compiler_params=_cparams(1),
        name="window_attn",
    )(sink, q, k, k, k, vt, vt, vt)


def _outproj_kernel(x_ref, o_ref, w_ref, g_ref, b_ref, y_ref, *, alpha):
    h = jnp.dot(o_ref[...], w_ref[...], preferred_element_type=F32)
    y_ref[...] = _layernorm_rows(alpha * x_ref[...] + h, g_ref[...], b_ref[...])


def _outproj(x, o, w, g, b, *, alpha, tm):
    t, d = x.shape
    k = o.shape[1]
    return pl.pallas_call(
        functools.partial(_outproj_kernel, alpha=alpha),
        grid=(t // tm,),
        in_specs=[
            pl.BlockSpec((tm, d), lambda i: (i, 0)),
            pl.BlockSpec((tm, k), lambda i: (i, 0)),
            pl.BlockSpec((k, d), lambda i: (0, 0)),
            pl.BlockSpec((1, d), lambda i: (0, 0)),
            pl.BlockSpec((1, d), lambda i: (0, 0)),
        ],
        out_specs=pl.BlockSpec((tm, d), lambda i: (i, 0)),
        out_shape=jax.ShapeDtypeStruct((t, d), F32),
        compiler_params=_cparams(1),
        name="outproj_ln",
    )(x, o, w, g, b)


def _largest_divisor(n, cap, multiple):
    best = None
    for c in range(multiple, min(n, cap) + 1, multiple):
        if n % c == 0:
            best = c
    assert best is not None, (n, cap, multiple)
    return best


def _tiles(seq_len, d_ff):
    return dict(
        tm=_largest_divisor(seq_len, 512, LANES),
        tmp=_largest_divisor(seq_len, 1024, LANES),
        tmf=_largest_divisor(seq_len, 1024, LANES),
        tf=_largest_divisor(d_ff, 512, 2 * LANES),
        tq=_largest_divisor(seq_len, 1024, LANES),
        tk=_largest_divisor(seq_len // 2, 512, LANES),
        tw=_largest_divisor(seq_len, 512, WG_WINDOW),
    )


def kernel(x_prompt, x_sample, ffn1_w_gate, ffn1_w_up, ffn1_w_down, ffn2_w_gate, ffn2_w_up, ffn2_w_down, ln_g, ln_b, da_w_qkv, da_w_o, da_lambda_q1, da_lambda_k1, da_lambda_q2, da_lambda_k2, da_subln_g, wg_w_qkv, wg_w_o, wg_sink):
    bp, sp, d = x_prompt.shape
    bs, ss, _ = x_sample.shape
    depth = ffn1_w_gate.shape[0]
    d_ff = ffn1_w_gate.shape[2]
    alpha = (2.0 * depth) ** 0.25

    da_hd = da_lambda_q1.shape[1]
    da_dv = da_subln_g.shape[1]
    da_heads = d // da_dv
    da_qd = 2 * da_heads * da_hd
    wg_heads = wg_sink.shape[1]
    wg_hd = d // wg_heads
    wg_kv = (wg_w_qkv.shape[2] - d) // (2 * wg_hd)
    assert 2 * da_hd == LANES and da_dv == LANES and wg_hd == LANES

    cos_a, sin_a = _rope_tables(max(sp, ss), da_hd)
    cos_b, sin_b = _rope_tables(max(sp, ss), wg_hd)

    ffn_w = [tuple(w.astype(BF16) for w in ws)
             for ws in ((ffn1_w_gate, ffn1_w_up, ffn1_w_down), (ffn2_w_gate, ffn2_w_up, ffn2_w_down))]
    mix_w = []
    for i in range(depth):
        j = i // 2
        if i % 2 == 0:
            w = da_w_qkv[j]
            mix_w.append((w[:, :2 * da_qd].astype(BF16), w[:, 2 * da_qd:].T.astype(BF16), da_w_o[j].astype(BF16)))
        else:
            w = wg_w_qkv[j].astype(BF16)
            kd = wg_kv * wg_hd
            mix_w.append((w[:, :d], w[:, d:d + kd], w[:, d + kd:], wg_w_o[j].astype(BF16)))

    def ln(i, k):
        return ln_g[i, k][None, :], ln_b[i, k][None, :]

    def trunk(x, seq_len):
        tl = _tiles(seq_len, d_ff)
        tm, tmp, tmf, tf, tq, tk, tw = (tl[n] for n in ("tm", "tmp", "tmf", "tf", "tq", "tk", "tw"))
        for i in range(depth):
            x = _ffn(x, *ffn_w[0], *ln(i, 0), layer=i, alpha=alpha, tm=tmf, tf=tf)
            j = i // 2
            if i % 2 == 0:
                lam_init = 0.8 - 0.6 * math.exp(-0.3 * i)
                w_qk, w_vt, w_o = mix_w[i]
                qk = _proj_rope(x, w_qk, cos_a, sin_a, head_dim=da_hd, n_scaled=da_qd // LANES,
                                scale=da_hd ** -0.5 * math.log2(math.e), tm=tmp, tn=da_qd // 2,
                                seq_len=seq_len)
                vt = _proj_t(x, w_vt, tm=tm, tk=tk, weight_is_transposed=True)
                o = _diff_attn(qk, vt, da_lambda_q1[j][None, :], da_lambda_k1[j][None, :],
                               da_lambda_q2[j][None, :], da_lambda_k2[j][None, :], da_subln_g[j][:, None],
                               seq_len=seq_len, n_heads=da_heads, tq=tq, tk=tk, lam_init=lam_init)
            else:
                w_q, w_k, w_v, w_o = mix_w[i]
                q = _proj_rope(x, w_q, cos_b, sin_b, head_dim=wg_hd, n_scaled=d // LANES,
                               scale=wg_hd ** -0.5 * math.log2(math.e), tm=tmp, tn=d // 2, seq_len=seq_len)
                k = _proj_rope(x, w_k, cos_b, sin_b, head_dim=wg_hd, n_scaled=0, scale=1.0, tm=tmp,
                               tn=wg_kv * wg_hd, seq_len=seq_len)
                vt = _proj_t(x, w_v, tm=tmp, tk=WG_WINDOW, weight_is_transposed=False)
                o = _window_attn(q, k, vt, wg_sink[j], n_heads=wg_heads, n_kv=wg_kv, tq=tw, seq_len=seq_len)
            x = _outproj(x, o, w_o, *ln(i, 1), alpha=alpha, tm=tm)
            x = _ffn(x, *ffn_w[1], *ln(i, 2), layer=i, alpha=alpha, tm=tmf, tf=tf)
        return x

    y_prompt = trunk(x_prompt.reshape(bp * sp, d), sp).reshape(bp, sp, d)
    y_sample = trunk(x_sample.reshape(bs * ss, d), ss).reshape(bs, ss, d)
    return (y_prompt, y_sample)
```

```python
import functools
import math

import jax
import jax.numpy as jnp
from jax import lax
from jax.experimental import pallas as pl
from jax.experimental.pallas import tpu as pltpu

F32 = jnp.float32
BF16 = jnp.bfloat16

LN_EPS = 1e-5
ROPE_THETA = 10000.0
WG_WINDOW = 128
LANES = 128
V7X_VMEM_LIMIT_BYTES = 60000 * 1024


def _cparams(n_axes):
    return pltpu.CompilerParams(
        dimension_semantics=("arbitrary",) * n_axes,
        vmem_limit_bytes=V7X_VMEM_LIMIT_BYTES,
    )


def _layernorm_rows(y, g, b):
    mu = jnp.mean(y, axis=-1, keepdims=True)
    yc = y - mu
    var = jnp.mean(yc * yc, axis=-1, keepdims=True)
    return yc * lax.rsqrt(var + LN_EPS) * g + b


FFN_ROW_SUBTILE = 512
LN_ROW_SUBTILE = 128


def _ffn_kernel(x_ref, wg_ref, wu_ref, wd_ref, g_ref, b_ref, o_ref, xb_ref, *, alpha, nf):
    f = pl.program_id(1)
    tm = xb_ref.shape[0]
    tr = min(FFN_ROW_SUBTILE, tm)
    tl = min(LN_ROW_SUBTILE, tr)

    def step(first, last):
        for r in range(0, tm, tr):
            if first:
                xb_ref[r:r + tr, :] = x_ref[r:r + tr, :].astype(BF16)
            xb = xb_ref[r:r + tr, :]
            hg = jnp.dot(xb, wg_ref[...], preferred_element_type=F32)
            hu = jnp.dot(xb, wu_ref[...], preferred_element_type=F32)
            a = (hg * jax.nn.sigmoid(hg) * hu).astype(BF16)
            part = jnp.dot(a, wd_ref[...], preferred_element_type=F32)
            if first:
                o_ref[r:r + tr, :] = part
            else:
                o_ref[r:r + tr, :] += part
            if last:
                for q in range(r, r + tr, tl):
                    y = alpha * x_ref[q:q + tl, :] + 0.5 * o_ref[q:q + tl, :]
                    o_ref[q:q + tl, :] = _layernorm_rows(y, g_ref[...], b_ref[...])

    if nf == 1:
        step(True, True)
    else:
        pl.when(f == 0)(lambda: step(True, False))
        pl.when((f > 0) & (f < nf - 1))(lambda: step(False, False))
        pl.when(f == nf - 1)(lambda: step(False, True))


def _ffn(x, wg, wu, wd, g, b, *, layer, alpha, tm, tf):
    t, d = x.shape
    f = wg.shape[2]
    nf = f // tf
    return pl.pallas_call(
        functools.partial(_ffn_kernel, alpha=alpha, nf=nf),
        grid=(t // tm, nf),
        in_specs=[
            pl.BlockSpec((tm, d), lambda i, j: (i, 0)),
            pl.BlockSpec((None, d, tf), lambda i, j: (layer, 0, j)),
            pl.BlockSpec((None, d, tf), lambda i, j: (layer, 0, j)),
            pl.BlockSpec((None, tf, d), lambda i, j: (layer, j, 0)),
            pl.BlockSpec((1, d), lambda i, j: (0, 0)),
            pl.BlockSpec((1, d), lambda i, j: (0, 0)),
        ],
        out_specs=pl.BlockSpec((tm, d), lambda i, j: (i, 0)),
        out_shape=jax.ShapeDtypeStruct((t, d), F32),
        scratch_shapes=[pltpu.VMEM((tm, d), BF16)],
        compiler_params=_cparams(2),
        name="ffn_ln",
    )(x, wg, wu, wd, g, b)


def _rope_tables(max_len, head_dim):
    inv = ROPE_THETA ** (-jnp.arange(0, head_dim, 2, dtype=F32) / head_dim)
    ang = jnp.arange(max_len, dtype=F32)[:, None] * inv[None, :]
    ang = jnp.concatenate([ang, ang], -1)
    cos = jnp.cos(ang)
    sin = jnp.sin(ang)
    half = head_dim // 2
    sign = jnp.where(jnp.arange(head_dim) < half, -1.0, 1.0).astype(F32)
    sin = sin * sign[None, :]
    reps = LANES // head_dim
    return jnp.tile(cos, (1, reps)), jnp.tile(sin, (1, reps))


def _rope_lane_tile(y, cos, sin_signed, head_dim):
    half = head_dim // 2
    lane = lax.broadcasted_iota(jnp.int32, y.shape, 1)
    lo = (lane & (head_dim - 1)) < half
    fwd = pltpu.roll(y, half, 1)
    bwd = pltpu.roll(y, LANES - half, 1)
    return y * cos + jnp.where(lo, bwd, fwd) * sin_signed


PROJ_ROW_SUBTILE = 512


def _proj_rope_kernel(x_ref, w_ref, cos_ref, sin_ref, o_ref, *, head_dim, n_scaled, scale):
    j = pl.program_id(1)
    tm = x_ref.shape[0]
    tr = min(PROJ_ROW_SUBTILE, tm)
    tiles_per_block = o_ref.shape[1] // LANES
    for r0 in range(0, tm, tr):
        y = jnp.dot(x_ref[r0:r0 + tr, :].astype(BF16), w_ref[...], preferred_element_type=F32)
        cos = cos_ref[r0:r0 + tr, :]
        sin = sin_ref[r0:r0 + tr, :]
        for c in range(tiles_per_block):
            r = _rope_lane_tile(y[:, c * LANES:(c + 1) * LANES], cos, sin, head_dim)
            tile_idx = j * tiles_per_block + c
            r = r * jnp.where(tile_idx < n_scaled, scale, 1.0).astype(F32)
            o_ref[r0:r0 + tr, c * LANES:(c + 1) * LANES] = r.astype(o_ref.dtype)


def _proj_rope(x, w, cos, sin, *, head_dim, n_scaled, scale, tm, tn, seq_len):
    t, d = x.shape
    n = w.shape[1]
    pos_spec = pl.BlockSpec((tm, LANES), lambda i, j: (i % (seq_len // tm), 0))
    return pl.pallas_call(
        functools.partial(_proj_rope_kernel, head_dim=head_dim, n_scaled=n_scaled, scale=scale),
        grid=(t // tm, n // tn),
        in_specs=[
            pl.BlockSpec((tm, d), lambda i, j: (i, 0)),
            pl.BlockSpec((d, tn), lambda i, j: (0, j)),
            pos_spec,
            pos_spec,
        ],
        out_specs=pl.BlockSpec((tm, tn), lambda i, j: (i, j)),
        out_shape=jax.ShapeDtypeStruct((t, n), BF16),
        compiler_params=_cparams(2),
        name="proj_rope",
    )(x, w, cos, sin)


def _proj_t_kernel(x_ref, wt_ref, o_ref, *, tk):
    yt = lax.dot_general(wt_ref[...], x_ref[...].astype(BF16), (((1,), (1,)), ((), ())),
                         preferred_element_type=F32)
    for c in range(o_ref.shape[0]):
        o_ref[c] = yt[:, c * tk:(c + 1) * tk].astype(o_ref.dtype)


def _proj_tr_kernel(x_ref, w_ref, o_ref, y_ref, *, tk):
    y_ref[...] = jnp.dot(x_ref[...].astype(BF16), w_ref[...], preferred_element_type=F32)
    for c in range(o_ref.shape[0]):
        o_ref[c] = y_ref[c * tk:(c + 1) * tk, :].T.astype(o_ref.dtype)


def _proj_t(x, w, *, tm, tk, weight_is_transposed):
    t, d = x.shape
    n = w.shape[0] if weight_is_transposed else w.shape[1]
    assert tm % tk == 0
    kernel = _proj_t_kernel if weight_is_transposed else _proj_tr_kernel
    return pl.pallas_call(
        functools.partial(kernel, tk=tk),
        grid=(t // tm,),
        in_specs=[
            pl.BlockSpec((tm, d), lambda i: (i, 0)),
            pl.BlockSpec(w.shape, lambda i: (0, 0)),
        ],
        out_specs=pl.BlockSpec((tm // tk, n, tk), lambda i: (i, 0, 0)),
        out_shape=jax.ShapeDtypeStruct((t // tk, n, tk), BF16),
        scratch_shapes=[] if weight_is_transposed else [pltpu.VMEM((tm, n), F32)],
        compiler_params=_cparams(1),
        name="proj_t",
    )(x, w)


ONES_ROWS = 16
DA_CHUNKS_PER_BODY = 4


def _diff_attn_kernel(q_ref, k_ref, vt_ref, lq1_ref, lk1_ref, lq2_ref, lk2_ref, g_ref, o_ref,
                      qb_ref, s_ref, cmax_ref, p_ref, a_ref, m_ref, acc_ref,
                      *, tq, tk, nk, group, hd, dv, lam_init):
    q = q_ref[...]
    lane = lax.broadcasted_iota(jnp.int32, q.shape, 1)
    zero = jnp.zeros_like(q)
    qb_ref[0:tq, :] = jnp.where(lane < hd, q, zero)
    qb_ref[tq:2 * tq, :] = jnp.where(lane >= hd, q, zero)
    ones = jnp.ones((ONES_ROWS, tk), BF16)

    def scores(i, slot):
        off = pl.multiple_of(i * tk, tk)
        k = k_ref[pl.ds(off, tk), :]
        s = lax.dot_general(k, qb_ref[...], (((1,), (1,)), ((), ())),
                            preferred_element_type=F32)
        s_ref[slot] = s
        cmax_ref[slot] = jnp.max(s, axis=0, keepdims=True)

    def softmax(slot):
        for c in range(2):
            m_old = m_ref[c]
            m_new = jnp.maximum(m_old, cmax_ref[slot, :, c * tq:(c + 1) * tq])
            a_ref[slot, c] = jnp.exp2(m_old - m_new)
            p_ref[slot, c] = jnp.exp2(s_ref[slot, :, c * tq:(c + 1) * tq] - m_new).astype(BF16)
            m_ref[c] = m_new

    def values(i, slot):
        vt = jnp.concatenate([vt_ref[i], ones], axis=0)
        for c in range(2):
            acc_ref[c] = a_ref[slot, c] * acc_ref[c] + jnp.dot(vt, p_ref[slot, c],
                                                               preferred_element_type=F32)

    m_ref[...] = jnp.full(m_ref.shape, -jnp.inf, F32)
    acc_ref[...] = jnp.zeros(acc_ref.shape, F32)

    def body(j, with_values=True, with_scores=True):
        if with_values:
            for g in range(group):
                values((j - 1) * group + g, g)
        for g in range(group):
            softmax(g)
        if with_scores:
            for g in range(group):
                scores((j + 1) * group + g, g)

    n_body = nk // group
    for g in range(group):
        scores(g, g)
    body(0, with_values=False)

    def loop_body(j, carry):
        body(j)
        return carry

    lax.fori_loop(1, n_body - 1, loop_body, 0)
    body(n_body - 1, with_scores=False)
    for g in range(group):
        values(nk - group + g, g)

    lam = (jnp.exp(jnp.sum(lq1_ref[...] * lk1_ref[...], axis=-1, keepdims=True))
           - jnp.exp(jnp.sum(lq2_ref[...] * lk2_ref[...], axis=-1, keepdims=True)) + lam_init)
    o = (acc_ref[0, 0:dv] * (1.0 / acc_ref[0, dv:dv + 1])
         - lam * (acc_ref[1, 0:dv] * (1.0 / acc_ref[1, dv:dv + 1])))
    ms = jnp.mean(o * o, axis=0, keepdims=True)
    o = o * lax.rsqrt(ms + LN_EPS) * g_ref[...] * (1.0 - lam_init)
    o_ref[...] = o.T.astype(o_ref.dtype)


def _diff_attn(qk, vt, lq1, lk1, lq2, lk2, g_col, *, seq_len, n_heads, tq, tk, lam_init):
    dv = g_col.shape[0]
    hd = lq1.shape[1]
    n_seq = qk.shape[0] // seq_len
    nq = seq_len // tq
    nk = seq_len // tk
    group = min(DA_CHUNKS_PER_BODY, nk // 2)
    assert nk % group == 0 and nk // group >= 2, "the chunk loop is software-pipelined in groups"
    kernel = functools.partial(_diff_attn_kernel, tq=tq, tk=tk, nk=nk, group=group, hd=hd, dv=dv,
                               lam_init=lam_init)
    vec = pl.BlockSpec((1, hd), lambda b, h, i: (0, 0))
    return pl.pallas_call(
        kernel,
        grid=(n_seq, n_heads, nq),
        in_specs=[
            pl.BlockSpec((tq, 2 * hd), lambda b, h, i: (b * nq + i, h)),
            pl.BlockSpec((seq_len, 2 * hd), lambda b, h, i: (b, n_heads + h)),
            pl.BlockSpec((nk, dv, tk), lambda b, h, i: (b, h, 0)),
            vec, vec, vec, vec,
            pl.BlockSpec((dv, 1), lambda b, h, i: (0, 0)),
        ],
        out_specs=pl.BlockSpec((tq, dv), lambda b, h, i: (b * nq + i, h)),
        out_shape=jax.ShapeDtypeStruct((qk.shape[0], n_heads * dv), BF16),
        scratch_shapes=[
            pltpu.VMEM((2 * tq, 2 * hd), BF16),
            pltpu.VMEM((group, tk, 2 * tq), F32),
            pltpu.VMEM((group, 1, 2 * tq), F32),
            pltpu.VMEM((group, 2, tk, tq), BF16),
            pltpu.VMEM((group, 2, 1, tq), F32),
            pltpu.VMEM((2, 1, tq), F32),
            pltpu.VMEM((2, dv + ONES_ROWS, tq), F32),
        ],
        compiler_params=_cparams(3),
        name="diff_attn",
    )(qk, qk, vt, lq1, lk1, lq2, lk2, g_col)


def _window_attn_kernel(sink_ref, q_ref, k_ref, kp_ref, kn_ref, vt_ref, vtp_ref, vtn_ref, o_ref,
                        kwin_ref, *, tq, n_kv, group, hd, seq_len):
    w = WG_WINDOW
    nb = tq // w
    pos0 = lax.rem(pl.program_id(0) * tq, jnp.int32(seq_len))

    kwin_ref[0:w, :] = kp_ref[...]
    kwin_ref[w:w + tq, :] = k_ref[...]
    kwin_ref[w + tq:w + tq + w, :] = kn_ref[...]

    def vt_block(j, g):
        ref, idx = (vtp_ref, 0) if j < 0 else (vtn_ref, 0) if j >= nb else (vt_ref, j)
        return ref[idx, g * hd:(g + 1) * hd, :]

    cols = group * w
    c_idx = lax.broadcasted_iota(jnp.int32, (3 * w, cols), 0)
    a_idx = lax.broadcasted_iota(jnp.int32, (3 * w, cols), 1) & (w - 1)
    rel = c_idx - a_idx
    band = (rel >= 0) & (rel <= 2 * w)
    head_in_group = lax.broadcasted_iota(jnp.int32, (1, cols), 1) // w
    ones = jnp.ones((ONES_ROWS, 3 * w), BF16)
    log2e = math.log2(math.e)

    key_row = lax.broadcasted_iota(jnp.int32, (3 * w, 1), 0)

    for n in range(nb):
        kpos = pos0 + (n - 1) * w + key_row
        valid = band & ((kpos >= 0) & (kpos < seq_len))
        for g in range(n_kv):
            kw = kwin_ref[n * w:n * w + 3 * w, g * hd:(g + 1) * hd]
            qg = jnp.concatenate(
                [q_ref[n * w:(n + 1) * w, (g * group + j) * hd:(g * group + j + 1) * hd]
                 for j in range(group)], axis=0)
            s = lax.dot_general(kw, qg, (((1,), (1,)), ((), ())), preferred_element_type=F32)
            s = jnp.where(valid, s, -jnp.inf)
            sk = jnp.zeros((1, cols), F32)
            for j in range(group):
                sk = jnp.where(head_in_group == j, sink_ref[g * group + j] * log2e, sk)
            m = jnp.maximum(jnp.max(s, axis=0, keepdims=True), sk)
            p = jnp.exp2(s - m).astype(BF16)
            vt = jnp.concatenate([vt_block(n - 1, g), vt_block(n, g), vt_block(n + 1, g)], axis=1)
            ot = jnp.dot(jnp.concatenate([vt, ones], axis=0), p, preferred_element_type=F32)
            den = ot[hd:hd + 1, :] + jnp.exp2(sk - m)
            o = (ot[0:hd, :] * (1.0 / den)).T
            for j in range(group):
                h = g * group + j
                o_ref[n * w:(n + 1) * w, h * hd:(h + 1) * hd] = o[j * w:(j + 1) * w, :].astype(o_ref.dtype)


def _window_attn(q, k, vt, sink, *, n_heads, n_kv, tq, seq_len):
    t, qd = q.shape
    hd = qd // n_heads
    kd = n_kv * hd
    w = WG_WINDOW
    r = tq // w
    n_blk = t // w
    kernel = functools.partial(_window_attn_kernel, tq=tq, n_kv=n_kv, group=n_heads // n_kv, hd=hd,
                               seq_len=seq_len)
    prev_blk = lambda i: jnp.maximum(i * r - 1, 0)
    next_blk = lambda i: jnp.minimum((i + 1) * r, n_blk - 1)
    return pl.pallas_call(
        kernel,
        grid=(t // tq,),
        in_specs=[
            pl.BlockSpec(memory_space=pltpu.SMEM),
            pl.BlockSpec((tq, qd), lambda i: (i, 0)),
            pl.BlockSpec((tq, kd), lambda i: (i, 0)),
            pl.BlockSpec((w, kd), lambda i: (prev_blk(i), 0)),
            pl.BlockSpec((w, kd), lambda i: (next_blk(i), 0)),
            pl.BlockSpec((r, kd, w), lambda i: (i, 0, 0)),
            pl.BlockSpec((1, kd, w), lambda i: (prev_blk(i), 0, 0)),
            pl.BlockSpec((1, kd, w), lambda i: (next_blk(i), 0, 0)),
        ],
        out_specs=pl.BlockSpec((tq, qd), lambda i: (i, 0)),
        scratch_shapes=[pltpu.VMEM((tq + 2 * w, kd), BF16)],
        out_shape=jax.ShapeDtypeStruct((t, qd), BF16),
        compiler_params=_cparams(1),
        name="window_attn",
    )(sink, q, k, k, k, vt, vt, vt)


OUTPROJ_ROW_SUBTILE = 256


def _outproj_kernel(x_ref, o_ref, w_ref, g_ref, b_ref, y_ref, *, alpha):
    tm = x_ref.shape[0]
    tr = min(OUTPROJ_ROW_SUBTILE, tm)
    for r in range(0, tm, tr):
        h = jnp.dot(o_ref[r:r + tr, :], w_ref[...], preferred_element_type=F32)
        y_ref[r:r + tr, :] = _layernorm_rows(alpha * x_ref[r:r + tr, :] + h, g_ref[...], b_ref[...])


def _outproj(x, o, w, g, b, *, alpha, tm):
    t, d = x.shape
    k = o.shape[1]
    return pl.pallas_call(
        functools.partial(_outproj_kernel, alpha=alpha),
        grid=(t // tm,),
        in_specs=[
            pl.BlockSpec((tm, d), lambda i: (i, 0)),
            pl.BlockSpec((tm, k), lambda i: (i, 0)),
            pl.BlockSpec((k, d), lambda i: (0, 0)),
            pl.BlockSpec((1, d), lambda i: (0, 0)),
            pl.BlockSpec((1, d), lambda i: (0, 0)),
        ],
        out_specs=pl.BlockSpec((tm, d), lambda i: (i, 0)),
        out_shape=jax.ShapeDtypeStruct((t, d), F32),
        compiler_params=_cparams(1),
        name="outproj_ln",
    )(x, o, w, g, b)


def _largest_divisor(n, cap, multiple):
    best = None
    for c in range(multiple, min(n, cap) + 1, multiple):
        if n % c == 0:
            best = c
    assert best is not None, (n, cap, multiple)
    return best


def _tiles(seq_len, d_ff):
    return dict(
        tm=_largest_divisor(seq_len, 512, LANES),
        tmp=_largest_divisor(seq_len, 1024, LANES),
        tmf=_largest_divisor(seq_len, 1024, LANES),
        tf=_largest_divisor(d_ff, 512, 2 * LANES),
        tq=_largest_divisor(seq_len, 1024, LANES),
        tk=_largest_divisor(seq_len // 2, 512, LANES),
        tw=_largest_divisor(seq_len, 512, WG_WINDOW),
    )


def kernel(x_prompt, x_sample, ffn1_w_gate, ffn1_w_up, ffn1_w_down, ffn2_w_gate, ffn2_w_up, ffn2_w_down, ln_g, ln_b, da_w_qkv, da_w_o, da_lambda_q1, da_lambda_k1, da_lambda_q2, da_lambda_k2, da_subln_g, wg_w_qkv, wg_w_o, wg_sink):
    bp, sp, d = x_prompt.shape
    bs, ss, _ = x_sample.shape
    depth = ffn1_w_gate.shape[0]
    d_ff = ffn1_w_gate.shape[2]
    alpha = (2.0 * depth) ** 0.25

    da_hd = da_lambda_q1.shape[1]
    da_dv = da_subln_g.shape[1]
    da_heads = d // da_dv
    da_qd = 2 * da_heads * da_hd
    wg_heads = wg_sink.shape[1]
    wg_hd = d // wg_heads
    wg_kv = (wg_w_qkv.shape[2] - d) // (2 * wg_hd)
    assert 2 * da_hd == LANES and da_dv == LANES and wg_hd == LANES

    cos_a, sin_a = _rope_tables(max(sp, ss), da_hd)
    cos_b, sin_b = _rope_tables(max(sp, ss), wg_hd)

    ffn_w = [tuple(w.astype(BF16) for w in ws)
             for ws in ((ffn1_w_gate, ffn1_w_up, ffn1_w_down), (ffn2_w_gate, ffn2_w_up, ffn2_w_down))]
    mix_w = []
    for i in range(depth):
        j = i // 2
        if i % 2 == 0:
            w = da_w_qkv[j]
            mix_w.append((w[:, :2 * da_qd].astype(BF16), w[:, 2 * da_qd:].T.astype(BF16), da_w_o[j].astype(BF16)))
        else:
            w = wg_w_qkv[j].astype(BF16)
            kd = wg_kv * wg_hd
            mix_w.append((w[:, :d], w[:, d:d + kd], w[:, d + kd:], wg_w_o[j].astype(BF16)))

    def ln(i, k):
        return ln_g[i, k][None, :], ln_b[i, k][None, :]

    def trunk(x, seq_len):
        tl = _tiles(seq_len, d_ff)
        tm, tmp, tmf, tf, tq, tk, tw = (tl[n] for n in ("tm", "tmp", "tmf", "tf", "tq", "tk", "tw"))
        for i in range(depth):
            x = _ffn(x, *ffn_w[0], *ln(i, 0), layer=i, alpha=alpha, tm=tmf, tf=tf)
            j = i // 2
            if i % 2 == 0:
                lam_init = 0.8 - 0.6 * math.exp(-0.3 * i)
                w_qk, w_vt, w_o = mix_w[i]
                qk = _proj_rope(x, w_qk, cos_a, sin_a, head_dim=da_hd, n_scaled=da_qd // LANES,
                                scale=da_hd ** -0.5 * math.log2(math.e), tm=tmp, tn=da_qd // 2,
                                seq_len=seq_len)
                vt = _proj_t(x, w_vt, tm=tm, tk=tk, weight_is_transposed=True)
                o = _diff_attn(qk, vt, da_lambda_q1[j][None, :], da_lambda_k1[j][None, :],
                               da_lambda_q2[j][None, :], da_lambda_k2[j][None, :], da_subln_g[j][:, None],
                               seq_len=seq_len, n_heads=da_heads, tq=tq, tk=tk, lam_init=lam_init)
            else:
                w_q, w_k, w_v, w_o = mix_w[i]
                q = _proj_rope(x, w_q, cos_b, sin_b, head_dim=wg_hd, n_scaled=d // LANES,
                               scale=wg_hd ** -0.5 * math.log2(math.e), tm=tmp, tn=d // 2, seq_len=seq_len)
                k = _proj_rope(x, w_k, cos_b, sin_b, head_dim=wg_hd, n_scaled=0, scale=1.0, tm=tmp,
                               tn=wg_kv * wg_hd, seq_len=seq_len)
                vt = _proj_t(x, w_v, tm=tmp, tk=WG_WINDOW, weight_is_transposed=False)
                o = _window_attn(q, k, vt, wg_sink[j], n_heads=wg_heads, n_kv=wg_kv, tq=tw, seq_len=seq_len)
            x = _outproj(x, o, w_o, *ln(i, 1), alpha=alpha, tm=tm)
            x = _ffn(x, *ffn_w[1], *ln(i, 2), layer=i, alpha=alpha, tm=tmf, tf=tf)
        return x

    y_prompt = trunk(x_prompt.reshape(bp * sp, d), sp).reshape(bp, sp, d)
    y_sample = trunk(x_sample.reshape(bs * ss, d), ss).reshape(bs, ss, d)
    return (y_prompt, y_sample)
```

```python
import functools
import math

import jax
import jax.numpy as jnp
from jax import lax
from jax.experimental import pallas as pl
from jax.experimental.pallas import tpu as pltpu

F32 = jnp.float32
BF16 = jnp.bfloat16

LN_EPS = 1e-5
ROPE_THETA = 10000.0
WG_WINDOW = 128
LANES = 128
V7X_VMEM_LIMIT_BYTES = 60000 * 1024


def _cparams(n_axes):
    return pltpu.CompilerParams(
        dimension_semantics=("arbitrary",) * n_axes,
        vmem_limit_bytes=V7X_VMEM_LIMIT_BYTES,
    )


def _layernorm_rows(y, g, b):
    mu = jnp.mean(y, axis=-1, keepdims=True)
    yc = y - mu
    var = jnp.mean(yc * yc, axis=-1, keepdims=True)
    return yc * lax.rsqrt(var + LN_EPS) * g + b


FFN_ROW_SUBTILE = 512
LN_ROW_SUBTILE = 128


def _ffn_kernel(x_ref, wg_ref, wu_ref, wd_ref, g_ref, b_ref, o_ref, xb_ref, *, alpha, nf):
    f = pl.program_id(1)
    tm = xb_ref.shape[0]
    tr = min(FFN_ROW_SUBTILE, tm)
    tl = min(LN_ROW_SUBTILE, tr)

    def step(first, last):
        for r in range(0, tm, tr):
            if first:
                xb_ref[r:r + tr, :] = x_ref[r:r + tr, :].astype(BF16)
            xb = xb_ref[r:r + tr, :]
            hg = jnp.dot(xb, wg_ref[...], preferred_element_type=F32)
            hu = jnp.dot(xb, wu_ref[...], preferred_element_type=F32)
            a = (hg * jax.nn.sigmoid(hg) * hu).astype(BF16)
            part = jnp.dot(a, wd_ref[...], preferred_element_type=F32)
            if first:
                o_ref[r:r + tr, :] = part
            else:
                o_ref[r:r + tr, :] += part
            if last:
                for q in range(r, r + tr, tl):
                    y = alpha * x_ref[q:q + tl, :] + 0.5 * o_ref[q:q + tl, :]
                    o_ref[q:q + tl, :] = _layernorm_rows(y, g_ref[...], b_ref[...])

    if nf == 1:
        step(True, True)
    else:
        pl.when(f == 0)(lambda: step(True, False))
        pl.when((f > 0) & (f < nf - 1))(lambda: step(False, False))
        pl.when(f == nf - 1)(lambda: step(False, True))


def _ffn(x, wg, wu, wd, g, b, *, layer, alpha, tm, tf):
    t, d = x.shape
    f = wg.shape[2]
    nf = f // tf
    return pl.pallas_call(
        functools.partial(_ffn_kernel, alpha=alpha, nf=nf),
        grid=(t // tm, nf),
        in_specs=[
            pl.BlockSpec((tm, d), lambda i, j: (i, 0)),
            pl.BlockSpec((None, d, tf), lambda i, j: (layer, 0, j)),
            pl.BlockSpec((None, d, tf), lambda i, j: (layer, 0, j)),
            pl.BlockSpec((None, tf, d), lambda i, j: (layer, j, 0)),
            pl.BlockSpec((1, d), lambda i, j: (0, 0)),
            pl.BlockSpec((1, d), lambda i, j: (0, 0)),
        ],
        out_specs=pl.BlockSpec((tm, d), lambda i, j: (i, 0)),
        out_shape=jax.ShapeDtypeStruct((t, d), F32),
        scratch_shapes=[pltpu.VMEM((tm, d), BF16)],
        compiler_params=_cparams(2),
        name="ffn_ln",
    )(x, wg, wu, wd, g, b)


def _rope_tables(max_len, head_dim):
    inv = ROPE_THETA ** (-jnp.arange(0, head_dim, 2, dtype=F32) / head_dim)
    ang = jnp.arange(max_len, dtype=F32)[:, None] * inv[None, :]
    ang = jnp.concatenate([ang, ang], -1)
    cos = jnp.cos(ang)
    sin = jnp.sin(ang)
    half = head_dim // 2
    sign = jnp.where(jnp.arange(head_dim) < half, -1.0, 1.0).astype(F32)
    sin = sin * sign[None, :]
    reps = LANES // head_dim
    return jnp.tile(cos, (1, reps)), jnp.tile(sin, (1, reps))


def _rope_lane_tile(y, cos, sin_signed, head_dim):
    half = head_dim // 2
    lane = lax.broadcasted_iota(jnp.int32, y.shape, 1)
    lo = (lane & (head_dim - 1)) < half
    fwd = pltpu.roll(y, half, 1)
    bwd = pltpu.roll(y, LANES - half, 1)
    return y * cos + jnp.where(lo, bwd, fwd) * sin_signed


PROJ_ROW_SUBTILE = 512


def _proj_rope_kernel(x_ref, w_ref, cos_ref, sin_ref, o_ref, *, head_dim, n_scaled, scale):
    j = pl.program_id(1)
    tm = x_ref.shape[0]
    tr = min(PROJ_ROW_SUBTILE, tm)
    tiles_per_block = o_ref.shape[1] // LANES
    for r0 in range(0, tm, tr):
        y = jnp.dot(x_ref[r0:r0 + tr, :].astype(BF16), w_ref[...], preferred_element_type=F32)
        cos = cos_ref[r0:r0 + tr, :]
        sin = sin_ref[r0:r0 + tr, :]
        for c in range(tiles_per_block):
            r = _rope_lane_tile(y[:, c * LANES:(c + 1) * LANES], cos, sin, head_dim)
            tile_idx = j * tiles_per_block + c
            r = r * jnp.where(tile_idx < n_scaled, scale, 1.0).astype(F32)
            o_ref[r0:r0 + tr, c * LANES:(c + 1) * LANES] = r.astype(o_ref.dtype)


def _proj_rope(x, w, cos, sin, *, head_dim, n_scaled, scale, tm, tn, seq_len):
    t, d = x.shape
    n = w.shape[1]
    pos_spec = pl.BlockSpec((tm, LANES), lambda i, j: (i % (seq_len // tm), 0))
    return pl.pallas_call(
        functools.partial(_proj_rope_kernel, head_dim=head_dim, n_scaled=n_scaled, scale=scale),
        grid=(t // tm, n // tn),
        in_specs=[
            pl.BlockSpec((tm, d), lambda i, j: (i, 0)),
            pl.BlockSpec((d, tn), lambda i, j: (0, j)),
            pos_spec,
            pos_spec,
        ],
        out_specs=pl.BlockSpec((tm, tn), lambda i, j: (i, j)),
        out_shape=jax.ShapeDtypeStruct((t, n), BF16),
        compiler_params=_cparams(2),
        name="proj_rope",
    )(x, w, cos, sin)


def _proj_t_kernel(x_ref, wt_ref, o_ref, *, tk):
    yt = lax.dot_general(wt_ref[...], x_ref[...].astype(BF16), (((1,), (1,)), ((), ())),
                         preferred_element_type=F32)
    for c in range(o_ref.shape[0]):
        o_ref[c] = yt[:, c * tk:(c + 1) * tk].astype(o_ref.dtype)


def _proj_tr_kernel(x_ref, w_ref, o_ref, y_ref, *, tk):
    y_ref[...] = jnp.dot(x_ref[...].astype(BF16), w_ref[...], preferred_element_type=F32)
    for c in range(o_ref.shape[0]):
        o_ref[c] = y_ref[c * tk:(c + 1) * tk, :].T.astype(o_ref.dtype)


def _proj_t(x, w, *, tm, tk, weight_is_transposed):
    t, d = x.shape
    n = w.shape[0] if weight_is_transposed else w.shape[1]
    assert tm % tk == 0
    kernel = _proj_t_kernel if weight_is_transposed else _proj_tr_kernel
    return pl.pallas_call(
        functools.partial(kernel, tk=tk),
        grid=(t // tm,),
        in_specs=[
            pl.BlockSpec((tm, d), lambda i: (i, 0)),
            pl.BlockSpec(w.shape, lambda i: (0, 0)),
        ],
        out_specs=pl.BlockSpec((tm // tk, n, tk), lambda i: (i, 0, 0)),
        out_shape=jax.ShapeDtypeStruct((t // tk, n, tk), BF16),
        scratch_shapes=[] if weight_is_transposed else [pltpu.VMEM((tm, n), F32)],
        compiler_params=_cparams(1),
        name="proj_t",
    )(x, w)


ONES_ROWS = 16
DA_CHUNKS_PER_BODY = 4


def _diff_attn_kernel(q_ref, k_ref, vt_ref, lq1_ref, lk1_ref, lq2_ref, lk2_ref, g_ref, o_ref,
                      qb_ref, s_ref, cmax_ref, p_ref, a_ref, m_ref, acc_ref,
                      *, tq, tk, nk, group, hd, dv, lam_init):
    q = q_ref[...]
    lane = lax.broadcasted_iota(jnp.int32, q.shape, 1)
    zero = jnp.zeros_like(q)
    qb_ref[0:tq, :] = jnp.where(lane < hd, q, zero)
    qb_ref[tq:2 * tq, :] = jnp.where(lane >= hd, q, zero)
    ones = jnp.ones((ONES_ROWS, tk), BF16)

    def scores(i, slot):
        off = pl.multiple_of(i * tk, tk)
        k = k_ref[pl.ds(off, tk), :]
        s = lax.dot_general(k, qb_ref[...], (((1,), (1,)), ((), ())),
                            preferred_element_type=F32)
        s_ref[slot] = s
        cmax_ref[slot] = jnp.max(s, axis=0, keepdims=True)

    def softmax(slot):
        for c in range(2):
            m_old = m_ref[c]
            m_new = jnp.maximum(m_old, cmax_ref[slot, :, c * tq:(c + 1) * tq])
            a_ref[slot, c] = jnp.exp2(m_old - m_new)
            p_ref[slot, c] = jnp.exp2(s_ref[slot, :, c * tq:(c + 1) * tq] - m_new).astype(BF16)
            m_ref[c] = m_new

    def values(i, slot):
        vt = jnp.concatenate([vt_ref[i], ones], axis=0)
        for c in range(2):
            acc_ref[c] = a_ref[slot, c] * acc_ref[c] + jnp.dot(vt, p_ref[slot, c],
                                                               preferred_element_type=F32)

    m_ref[...] = jnp.full(m_ref.shape, -jnp.inf, F32)
    acc_ref[...] = jnp.zeros(acc_ref.shape, F32)

    def body(j, with_values=True, with_scores=True):
        if with_values:
            for g in range(group):
                values((j - 1) * group + g, g)
        for g in range(group):
            softmax(g)
        if with_scores:
            for g in range(group):
                scores((j + 1) * group + g, g)

    n_body = nk // group
    for g in range(group):
        scores(g, g)
    body(0, with_values=False)

    def loop_body(j, carry):
        body(j)
        return carry

    lax.fori_loop(1, n_body - 1, loop_body, 0)
    body(n_body - 1, with_scores=False)
    for g in range(group):
        values(nk - group + g, g)

    lam = (jnp.exp(jnp.sum(lq1_ref[...] * lk1_ref[...], axis=-1, keepdims=True))
           - jnp.exp(jnp.sum(lq2_ref[...] * lk2_ref[...], axis=-1, keepdims=True)) + lam_init)
    o = (acc_ref[0, 0:dv] * (1.0 / acc_ref[0, dv:dv + 1])
         - lam * (acc_ref[1, 0:dv] * (1.0 / acc_ref[1, dv:dv + 1])))
    ms = jnp.mean(o * o, axis=0, keepdims=True)
    o = o * lax.rsqrt(ms + LN_EPS) * g_ref[...] * (1.0 - lam_init)
    o_ref[...] = o.T.astype(o_ref.dtype)


def _diff_attn(qk, vt, lq1, lk1, lq2, lk2, g_col, *, seq_len, n_heads, tq, tk, lam_init):
    dv = g_col.shape[0]
    hd = lq1.shape[1]
    n_seq = qk.shape[0] // seq_len
    nq = seq_len // tq
    nk = seq_len // tk
    group = min(DA_CHUNKS_PER_BODY, nk // 2)
    assert nk % group == 0 and nk // group >= 2, "the chunk loop is software-pipelined in groups"
    kernel = functools.partial(_diff_attn_kernel, tq=tq, tk=tk, nk=nk, group=group, hd=hd, dv=dv,
                               lam_init=lam_init)
    vec = pl.BlockSpec((1, hd), lambda b, h, i: (0, 0))
    return pl.pallas_call(
        kernel,
        grid=(n_seq, n_heads, nq),
        in_specs=[
            pl.BlockSpec((tq, 2 * hd), lambda b, h, i: (b * nq + i, h)),
            pl.BlockSpec((seq_len, 2 * hd), lambda b, h, i: (b, n_heads + h)),
            pl.BlockSpec((nk, dv, tk), lambda b, h, i: (b, h, 0)),
            vec, vec, vec, vec,
            pl.BlockSpec((dv, 1), lambda b, h, i: (0, 0)),
        ],
        out_specs=pl.BlockSpec((tq, dv), lambda b, h, i: (b * nq + i, h)),
        out_shape=jax.ShapeDtypeStruct((qk.shape[0], n_heads * dv), BF16),
        scratch_shapes=[
            pltpu.VMEM((2 * tq, 2 * hd), BF16),
            pltpu.VMEM((group, tk, 2 * tq), F32),
            pltpu.VMEM((group, 1, 2 * tq), F32),
            pltpu.VMEM((group, 2, tk, tq), BF16),
            pltpu.VMEM((group, 2, 1, tq), F32),
            pltpu.VMEM((2, 1, tq), F32),
            pltpu.VMEM((2, dv + ONES_ROWS, tq), F32),
        ],
        compiler_params=_cparams(3),
        name="diff_attn",
    )(qk, qk, vt, lq1, lk1, lq2, lk2, g_col)


def _window_attn_kernel(sink_ref, q_ref, k_ref, kp_ref, kn_ref, vt_ref, vtp_ref, vtn_ref, o_ref,
                        kwin_ref, *, tq, n_kv, group, hd, seq_len):
    w = WG_WINDOW
    nb = tq // w
    pos0 = lax.rem(pl.program_id(0) * tq, jnp.int32(seq_len))

    kwin_ref[0:w, :] = kp_ref[...]
    kwin_ref[w:w + tq, :] = k_ref[...]
    kwin_ref[w + tq:w + tq + w, :] = kn_ref[...]

    def vt_block(j, g):
        ref, idx = (vtp_ref, 0) if j < 0 else (vtn_ref, 0) if j >= nb else (vt_ref, j)
        return ref[idx, g * hd:(g + 1) * hd, :]

    cols = group * w
    c_idx = lax.broadcasted_iota(jnp.int32, (3 * w, cols), 0)
    a_idx = lax.broadcasted_iota(jnp.int32, (3 * w, cols), 1) & (w - 1)
    rel = c_idx - a_idx
    band = (rel >= 0) & (rel <= 2 * w)
    head_in_group = lax.broadcasted_iota(jnp.int32, (1, cols), 1) // w
    ones = jnp.ones((ONES_ROWS, 3 * w), BF16)
    log2e = math.log2(math.e)

    key_row = lax.broadcasted_iota(jnp.int32, (3 * w, 1), 0)

    for n in range(nb):
        kpos = pos0 + (n - 1) * w + key_row
        valid = band & ((kpos >= 0) & (kpos < seq_len))
        for g in range(n_kv):
            kw = kwin_ref[n * w:n * w + 3 * w, g * hd:(g + 1) * hd]
            qg = jnp.concatenate(
                [q_ref[n * w:(n + 1) * w, (g * group + j) * hd:(g * group + j + 1) * hd]
                 for j in range(group)], axis=0)
            s = lax.dot_general(kw, qg, (((1,), (1,)), ((), ())), preferred_element_type=F32)
            s = jnp.where(valid, s, -jnp.inf)
            sk = jnp.zeros((1, cols), F32)
            for j in range(group):
                sk = jnp.where(head_in_group == j, sink_ref[g * group + j] * log2e, sk)
            m = jnp.maximum(jnp.max(s, axis=0, keepdims=True), sk)
            p = jnp.exp2(s - m).astype(BF16)
            vt = jnp.concatenate([vt_block(n - 1, g), vt_block(n, g), vt_block(n + 1, g)], axis=1)
            ot = jnp.dot(jnp.concatenate([vt, ones], axis=0), p, preferred_element_type=F32)
            den = ot[hd:hd + 1, :] + jnp.exp2(sk - m)
            o = (ot[0:hd, :] * (1.0 / den)).T
            for j in range(group):
                h = g * group + j
                o_ref[n * w:(n + 1) * w, h * hd:(h + 1) * hd] = o[j * w:(j + 1) * w, :].astype(o_ref.dtype)


def _window_attn(qk, vt, sink, *, n_heads, n_kv, hd, tq, seq_len):
    t = qk.shape[0]
    qd = n_heads * hd
    kd = n_kv * hd
    assert qd % kd == 0
    k_col = qd // kd
    w = WG_WINDOW
    r = tq // w
    n_blk = t // w
    kernel = functools.partial(_window_attn_kernel, tq=tq, n_kv=n_kv, group=n_heads // n_kv, hd=hd,
                               seq_len=seq_len)
    prev_blk = lambda i: jnp.maximum(i * r - 1, 0)
    next_blk = lambda i: jnp.minimum((i + 1) * r, n_blk - 1)
    return pl.pallas_call(
        kernel,
        grid=(t // tq,),
        in_specs=[
            pl.BlockSpec(memory_space=pltpu.SMEM),
            pl.BlockSpec((tq, qd), lambda i: (i, 0)),
            pl.BlockSpec((tq, kd), lambda i: (i, k_col)),
            pl.BlockSpec((w, kd), lambda i: (prev_blk(i), k_col)),
            pl.BlockSpec((w, kd), lambda i: (next_blk(i), k_col)),
            pl.BlockSpec((r, kd, w), lambda i: (i, 0, 0)),
            pl.BlockSpec((1, kd, w), lambda i: (prev_blk(i), 0, 0)),
            pl.BlockSpec((1, kd, w), lambda i: (next_blk(i), 0, 0)),
        ],
        out_specs=pl.BlockSpec((tq, qd), lambda i: (i, 0)),
        scratch_shapes=[pltpu.VMEM((tq + 2 * w, kd), BF16)],
        out_shape=jax.ShapeDtypeStruct((t, qd), BF16),
        compiler_params=_cparams(1),
        name="window_attn",
    )(sink, qk, qk, qk, qk, vt, vt, vt)


def _outproj_kernel(x_ref, o_ref, w_ref, g_ref, b_ref, y_ref, *, alpha):
    h = jnp.dot(o_ref[...], w_ref[...], preferred_element_type=F32)
    y_ref[...] = _layernorm_rows(alpha * x_ref[...] + h, g_ref[...], b_ref[...])


def _outproj(x, o, w, g, b, *, alpha, tm):
    t, d = x.shape
    k = o.shape[1]
    return pl.pallas_call(
        functools.partial(_outproj_kernel, alpha=alpha),
        grid=(t // tm,),
        in_specs=[
            pl.BlockSpec((tm, d), lambda i: (i, 0)),
            pl.BlockSpec((tm, k), lambda i: (i, 0)),
            pl.BlockSpec((k, d), lambda i: (0, 0)),
            pl.BlockSpec((1, d), lambda i: (0, 0)),
            pl.BlockSpec((1, d), lambda i: (0, 0)),
        ],
        out_specs=pl.BlockSpec((tm, d), lambda i: (i, 0)),
        out_shape=jax.ShapeDtypeStruct((t, d), F32),
        compiler_params=_cparams(1),
        name="outproj_ln",
    )(x, o, w, g, b)


def _largest_divisor(n, cap, multiple):
    best = None
    for c in range(multiple, min(n, cap) + 1, multiple):
        if n % c == 0:
            best = c
    assert best is not None, (n, cap, multiple)
    return best


def _tiles(seq_len, d_ff):
    return dict(
        tm=_largest_divisor(seq_len, 512, LANES),
        tmp=_largest_divisor(seq_len, 1024, LANES),
        tmf=_largest_divisor(seq_len, 1024, LANES),
        tf=_largest_divisor(d_ff, 512, 2 * LANES),
        tq=_largest_divisor(seq_len, 1024, LANES),
        tk=_largest_divisor(seq_len // 2, 512, LANES),
        tw=_largest_divisor(seq_len, 512, WG_WINDOW),
    )


def kernel(x_prompt, x_sample, ffn1_w_gate, ffn1_w_up, ffn1_w_down, ffn2_w_gate, ffn2_w_up, ffn2_w_down, ln_g, ln_b, da_w_qkv, da_w_o, da_lambda_q1, da_lambda_k1, da_lambda_q2, da_lambda_k2, da_subln_g, wg_w_qkv, wg_w_o, wg_sink):
    bp, sp, d = x_prompt.shape
    bs, ss, _ = x_sample.shape
    depth = ffn1_w_gate.shape[0]
    d_ff = ffn1_w_gate.shape[2]
    alpha = (2.0 * depth) ** 0.25

    da_hd = da_lambda_q1.shape[1]
    da_dv = da_subln_g.shape[1]
    da_heads = d // da_dv
    da_qd = 2 * da_heads * da_hd
    wg_heads = wg_sink.shape[1]
    wg_hd = d // wg_heads
    wg_kv = (wg_w_qkv.shape[2] - d) // (2 * wg_hd)
    assert 2 * da_hd == LANES and da_dv == LANES and wg_hd == LANES

    cos_a, sin_a = _rope_tables(max(sp, ss), da_hd)
    cos_b, sin_b = _rope_tables(max(sp, ss), wg_hd)

    ffn_w = [tuple(w.astype(BF16) for w in ws)
             for ws in ((ffn1_w_gate, ffn1_w_up, ffn1_w_down), (ffn2_w_gate, ffn2_w_up, ffn2_w_down))]
    mix_w = []
    for i in range(depth):
        j = i // 2
        if i % 2 == 0:
            w = da_w_qkv[j]
            mix_w.append((w[:, :2 * da_qd].astype(BF16), w[:, 2 * da_qd:].T.astype(BF16), da_w_o[j].astype(BF16)))
        else:
            w = wg_w_qkv[j].astype(BF16)
            kd = wg_kv * wg_hd
            mix_w.append((w[:, :d + kd], w[:, d + kd:], wg_w_o[j].astype(BF16)))

    def ln(i, k):
        return ln_g[i, k][None, :], ln_b[i, k][None, :]

    def trunk(x, seq_len):
        tl = _tiles(seq_len, d_ff)
        tm, tmp, tmf, tf, tq, tk, tw = (tl[n] for n in ("tm", "tmp", "tmf", "tf", "tq", "tk", "tw"))
        for i in range(depth):
            x = _ffn(x, *ffn_w[0], *ln(i, 0), layer=i, alpha=alpha, tm=tmf, tf=tf)
            j = i // 2
            if i % 2 == 0:
                lam_init = 0.8 - 0.6 * math.exp(-0.3 * i)
                w_qk, w_vt, w_o = mix_w[i]
                qk = _proj_rope(x, w_qk, cos_a, sin_a, head_dim=da_hd, n_scaled=da_qd // LANES,
                                scale=da_hd ** -0.5 * math.log2(math.e), tm=tmp, tn=da_qd // 2,
                                seq_len=seq_len)
                vt = _proj_t(x, w_vt, tm=tm, tk=tk, weight_is_transposed=True)
                o = _diff_attn(qk, vt, da_lambda_q1[j][None, :], da_lambda_k1[j][None, :],
                               da_lambda_q2[j][None, :], da_lambda_k2[j][None, :], da_subln_g[j][:, None],
                               seq_len=seq_len, n_heads=da_heads, tq=tq, tk=tk, lam_init=lam_init)
            else:
                w_qk, w_v, w_o = mix_w[i]
                qk = _proj_rope(x, w_qk, cos_b, sin_b, head_dim=wg_hd, n_scaled=d // LANES,
                                scale=wg_hd ** -0.5 * math.log2(math.e), tm=tmp, tn=wg_kv * wg_hd,
                                seq_len=seq_len)
                vt = _proj_t(x, w_v, tm=tmp, tk=WG_WINDOW, weight_is_transposed=False)
                o = _window_attn(qk, vt, wg_sink[j], n_heads=wg_heads, n_kv=wg_kv, hd=wg_hd, tq=tw,
                                 seq_len=seq_len)
            x = _outproj(x, o, w_o, *ln(i, 1), alpha=alpha, tm=tm)
            x = _ffn(x, *ffn_w[1], *ln(i, 2), layer=i, alpha=alpha, tm=tmf, tf=tf)
        return x

    y_prompt = trunk(x_prompt.reshape(bp * sp, d), sp).reshape(bp, sp, d)
    y_sample = trunk(x_sample.reshape(bs * ss, d), ss).reshape(bs, ss, d)
    return (y_prompt, y_sample)
```
